```python
import math
import jax, jax.numpy as jnp
from jax import lax
import numpy as np

D_MODEL = 1024
BATCH = 32
SEQ = 2048
DEPTH = 4
DEC_BATCH = 16
DEC_SEQ = 32
PAST_LEN = 1024

CHUNK = 64
N_MIXERS = 3
N_GDN = (DEPTH + 2) // 3
N_MLA = (DEPTH + 1) // 3
N_SWA = DEPTH // 3
EPS = 1e-6

GDN_HEADS = 8
GDN_DK = 128
GDN_DV = 128
GDN_CONV = 4
GDN_QKV = GDN_HEADS * (2 * GDN_DK + GDN_DV)
GDN_IN = GDN_QKV + GDN_HEADS * GDN_DV + 2 * GDN_HEADS

MLA_HEADS = 16
MLA_Q_LORA = 384
MLA_KV_LORA = 256
MLA_NOPE = 64
MLA_ROPE = 32
MLA_V = 64
MLA_IN = MLA_Q_LORA + MLA_KV_LORA + MLA_ROPE
ROPE_THETA = 10000.0
Q_BLOCK = 128

SWA_HEADS = 16
SWA_KV_HEADS = 4
SWA_GROUP = SWA_HEADS // SWA_KV_HEADS
SWA_HD = 64
WINDOW = 128
WIN_CHUNKS = WINDOW // CHUNK
SWA_IN = (SWA_HEADS + 2 * SWA_KV_HEADS) * SWA_HD

REL_BUCKETS = 32
REL_MAX_DIST = 128

D_FF = 2816
FFN_CONV = 3

kernel_name = 'hybrid_streaming_encoder_step'

F32 = jnp.float32


def rmsnorm(x, g):
    xf = x.astype(F32)
    y = xf * lax.rsqrt(jnp.mean(xf * xf, axis=-1, keepdims=True) + EPS)
    return (y * g.astype(F32)).astype(x.dtype)


def l2norm(x):
    xf = x.astype(F32)
    return xf * lax.rsqrt(jnp.sum(xf * xf, axis=-1, keepdims=True) + EPS)


def causal_dwconv(x, hist, w):
    K = w.shape[0]
    T = x.shape[1]
    xp = jnp.concatenate([hist.astype(x.dtype), x], axis=1)
    y = xp[:, 0:T] * w[0]
    for j in range(1, K):
        y = y + xp[:, j:j + T] * w[j]
    return y, xp[:, -(K - 1):]


def rope(x, pos):
    half = x.shape[-1] // 2
    inv = ROPE_THETA ** (-jnp.arange(half, dtype=F32) / half)
    ang = pos.astype(F32)[:, None] * inv[None, :]
    if x.ndim == 4:
        ang = ang[:, None, :]
    cos, sin = jnp.cos(ang), jnp.sin(ang)
    xf = x.astype(F32)
    x1, x2 = xf[..., :half], xf[..., half:]
    return jnp.concatenate([x1 * cos - x2 * sin, x1 * sin + x2 * cos], axis=-1).astype(x.dtype)


def t5_bias(table, q_pos, k_pos):
    n = q_pos[:, None] - k_pos[None, :]
    half = REL_BUCKETS // 2
    exact = half // 2
    side = jnp.where(n < 0, half, 0)
    n = jnp.abs(n)
    log_b = exact + (jnp.log(jnp.maximum(n, 1).astype(F32) / exact)
                     / math.log(REL_MAX_DIST / exact) * (half - exact)).astype(jnp.int32)
    bucket = side + jnp.where(n < exact, n, jnp.minimum(log_b, half - 1))
    bias = table[bucket].astype(F32).transpose(2, 0, 1)
    return bias.reshape(SWA_KV_HEADS, SWA_GROUP, q_pos.shape[0], k_pos.shape[0])


def gated_delta_rule(q, k, v, g, beta, S0, chunk):
    B, T, H, dk = q.shape
    dv = v.shape[-1]
    nc = T // chunk

    def blk4(a):
        return a.astype(F32).reshape(B, nc, chunk, H, a.shape[-1]).transpose(1, 0, 3, 2, 4)

    def blk3(a):
        return a.astype(F32).reshape(B, nc, chunk, H).transpose(1, 0, 3, 2)

    qc, kc, vc = blk4(q) * (dk ** -0.5), blk4(k), blk4(v)
    gc = jnp.cumsum(blk3(g), axis=-1)
    bc = blk3(beta)
    idx = jnp.arange(chunk)
    causal = idx[:, None] >= idx[None, :]
    decay = jnp.exp(jnp.where(causal, gc[..., :, None] - gc[..., None, :], -jnp.inf))
    kb = kc * bc[..., None]
    A = jnp.where(idx[:, None] > idx[None, :],
                  jnp.einsum('nbhid,nbhjd->nbhij', kb, kc) * decay, 0.0)
    eye = jnp.eye(chunk, dtype=F32)
    rhs = jnp.concatenate([vc * bc[..., None], kb * jnp.exp(gc)[..., None]], axis=-1)
    sol = lax.linalg.triangular_solve(eye + A, rhs, left_side=True, lower=True, unit_diagonal=True)
    w_v, w_k = sol[..., :dv], sol[..., dv:]
    qk = jnp.einsum('nbhid,nbhjd->nbhij', qc, kc) * decay

    def step(S, inp):
        qi, ki, wv, wk, qki, gi = inp
        u = wv - jnp.einsum('bhik,bhkv->bhiv', wk, S)
        o = (jnp.einsum('bhik,bhkv->bhiv', qi * jnp.exp(gi)[..., None], S)
             + jnp.einsum('bhij,bhjv->bhiv', qki, u))
        gl = gi[..., -1:]
        S = (S * jnp.exp(gl)[..., None]
             + jnp.einsum('bhik,bhiv->bhkv', ki * jnp.exp(gl - gi)[..., None], u))
        return S, o

    S, o = lax.scan(step, S0, (qc, kc, w_v, w_k, qk, gc))
    o = o.transpose(1, 0, 3, 2, 4).reshape(B, T, H, dv)
    return o, S


def gdn_mixer(h, hist, S0, w_in, conv_w, a_log, dt_bias, o_norm, w_o, chunk):
    B, T, _ = h.shape
    H, dk, dv = GDN_HEADS, GDN_DK, GDN_DV
    proj = h @ w_in
    qkv = proj[..., :GDN_QKV]
    z = proj[..., GDN_QKV:GDN_QKV + H * dv].reshape(B, T, H, dv)
    a = proj[..., GDN_QKV + H * dv:GDN_QKV + H * dv + H]
    b = proj[..., GDN_QKV + H * dv + H:]
    conv, new_hist = causal_dwconv(qkv, hist, conv_w)
    conv = jax.nn.silu(conv)
    q = l2norm(conv[..., :H * dk].reshape(B, T, H, dk))
    k = l2norm(conv[..., H * dk:2 * H * dk].reshape(B, T, H, dk))
    v = conv[..., 2 * H * dk:].reshape(B, T, H, dv)
    g = -jnp.exp(a_log.astype(F32)) * jax.nn.softplus(a.astype(F32) + dt_bias.astype(F32))
    beta = jax.nn.sigmoid(b.astype(F32))
    o, S = gated_delta_rule(q, k, v, g, beta, S0.astype(F32), chunk)
    o = rmsnorm(o, o_norm) * jax.nn.silu(z.astype(F32))
    out = o.reshape(B, T, H * dv).astype(h.dtype) @ w_o
    return out, new_hist, S.astype(h.dtype)


def mla_project(h, pos, w_in, q_norm, kv_norm, w_q_up):
    B, T, _ = h.shape
    proj = h @ w_in
    cq = rmsnorm(proj[..., :MLA_Q_LORA], q_norm)
    ckv = rmsnorm(proj[..., MLA_Q_LORA:MLA_Q_LORA + MLA_KV_LORA], kv_norm)
    kr = rope(proj[..., MLA_Q_LORA + MLA_KV_LORA:], pos)
    q = (cq @ w_q_up).reshape(B, T, MLA_HEADS, MLA_NOPE + MLA_ROPE)
    return q[..., :MLA_NOPE], rope(q[..., MLA_NOPE:], pos), ckv, kr


def mla_attend(qn, qr, q_pos, ckv, kr, k_pos, w_kv_up, w_o):
    B, Tq = qn.shape[:2]
    Tk = ckv.shape[1]
    kv = (ckv @ w_kv_up).reshape(B, Tk, MLA_HEADS, MLA_NOPE + MLA_V)
    k_nope, v = kv[..., :MLA_NOPE], kv[..., MLA_NOPE:]
    scale = (MLA_NOPE + MLA_ROPE) ** -0.5
    k_chunk = k_pos // CHUNK

    def block(args):
        qnb, qrb, qpb = args
        s = (jnp.einsum('bqhd,bkhd->bhqk', qnb, k_nope)
             + jnp.einsum('bqhr,bkr->bhqk', qrb, kr)).astype(F32) * scale
        mask = k_chunk[None, :] <= (qpb // CHUNK)[:, None]
        p = jax.nn.softmax(jnp.where(mask, s, -jnp.inf), axis=-1)
        return jnp.einsum('bhqk,bkhd->bqhd', p.astype(v.dtype), v)

    qb = min(Q_BLOCK, Tq)
    nb = Tq // qb
    qn_b = qn.reshape(B, nb, qb, MLA_HEADS, MLA_NOPE).transpose(1, 0, 2, 3, 4)
    qr_b = qr.reshape(B, nb, qb, MLA_HEADS, MLA_ROPE).transpose(1, 0, 2, 3, 4)
    o = lax.map(block, (qn_b, qr_b, q_pos.reshape(nb, qb)))
    o = o.transpose(1, 0, 2, 3, 4).reshape(B, Tq, MLA_HEADS * MLA_V)
    return o @ w_o


def swa_project(h, w_in):
    B, T, _ = h.shape
    proj = h @ w_in
    nq, nk = SWA_HEADS * SWA_HD, SWA_KV_HEADS * SWA_HD
    q = proj[..., :nq].reshape(B, T, SWA_KV_HEADS, SWA_GROUP, SWA_HD)
    k = proj[..., nq:nq + nk].reshape(B, T, SWA_KV_HEADS, SWA_HD)
    v = proj[..., nq + nk:].reshape(B, T, SWA_KV_HEADS, SWA_HD)
    return q, k, v


def sink_attend(q, k, v, bias, mask, sinks):
    s = jnp.einsum('bnqhgd,bnshd->bnhgqs', q, k).astype(F32) * (SWA_HD ** -0.5) + bias
    s = jnp.where(mask[:, None, None], s, -jnp.inf)
    sink = sinks.astype(F32).reshape(SWA_KV_HEADS, SWA_GROUP, 1, 1)
    m = jnp.maximum(jnp.max(s, axis=-1, keepdims=True), sink)
    p = jnp.exp(s - m)
    p = p / (jnp.sum(p, axis=-1, keepdims=True) + jnp.exp(sink - m))
    return jnp.einsum('bnhgqs,bnshd->bnqhgd', p.astype(v.dtype), v)


def band(a):
    B, T = a.shape[:2]
    nc = T // CHUNK
    ac = a.reshape(B, nc, CHUNK, SWA_KV_HEADS, SWA_HD)
    ap = jnp.pad(ac, ((0, 0), (WIN_CHUNKS, 0), (0, 0), (0, 0), (0, 0)))
    return jnp.concatenate([ap[:, j:j + nc] for j in range(WIN_CHUNKS + 1)], axis=2)


def swa_prompt(q, k, v, rel_bias, sinks):
    B, T = q.shape[:2]
    nc = T // CHUNK
    span = (WIN_CHUNKS + 1) * CHUNK
    qb = q.reshape(B, nc, CHUNK, SWA_KV_HEADS, SWA_GROUP, SWA_HD)
    q_rel = WIN_CHUNKS * CHUNK + jnp.arange(CHUNK)
    k_rel = jnp.arange(span)
    bias = t5_bias(rel_bias, q_rel, k_rel)
    k_chunk = jnp.arange(nc)[:, None] - WIN_CHUNKS + (k_rel // CHUNK)[None, :]
    mask = jnp.broadcast_to((k_chunk >= 0)[:, None, :], (nc, CHUNK, span))
    o = sink_attend(qb, band(k), band(v), bias, mask, sinks)
    return o.reshape(B, T, SWA_HEADS * SWA_HD)


def swa_sample(q, q_pos, k_all, v_all, k_pos, rel_bias, sinks):
    B, T = q.shape[:2]
    d = (q_pos // CHUNK)[:, None] - (k_pos // CHUNK)[None, :]
    mask = ((d >= 0) & (d <= WIN_CHUNKS))[None]
    bias = t5_bias(rel_bias, q_pos, k_pos)
    o = sink_attend(q[:, None], k_all[:, None], v_all[:, None], bias, mask, sinks)
    return o.reshape(B, T, SWA_HEADS * SWA_HD)


def conv_ffn(h, hist, w_in, conv_w, conv_b, w_out):
    proj = h @ w_in
    gate, up = proj[..., :D_FF], proj[..., D_FF:]
    gate_c, new_hist = causal_dwconv(gate, hist, conv_w)
    return (jax.nn.silu(gate_c + conv_b) * up) @ w_out, new_hist


def trunk(x, c, prompt, past_len, st, p):
    B, T, _ = x.shape
    pos = jnp.arange(T, dtype=jnp.int32) + (0 if prompt else past_len)
    new = dict(gdn_conv=[], gdn_S=[], mla_latent=[], mla_krope=[], swa_k=[], swa_v=[], ffn_conv=[])
    cs = jax.nn.silu(c)
    for layer in range(DEPTH):
        kind, slot = layer % N_MIXERS, layer // N_MIXERS
        mod = cs @ p['ada_w'][layer] + p['ada_b'][layer]
        sh1, sc1, g1, sh2, sc2, g2 = jnp.split(mod[:, None, :], 6, axis=-1)
        h = rmsnorm(x, p['norm1'][layer]) * (1 + sc1) + sh1
        if kind == 0:
            mix, conv_h, S = gdn_mixer(h, st['gdn_conv'][slot], st['gdn_S'][slot], p['gdn_w_in'][slot],
                                       p['gdn_conv_w'][slot], p['gdn_a_log'][slot], p['gdn_dt_bias'][slot],
                                       p['gdn_o_norm'][slot], p['gdn_w_o'][slot], CHUNK if prompt else T)
            new['gdn_conv'].append(conv_h)
            new['gdn_S'].append(S)
        elif kind == 1:
            qn, qr, ckv, kr = mla_project(h, pos, p['mla_w_in'][slot], p['mla_q_norm'][slot],
                                          p['mla_kv_norm'][slot], p['mla_w_q_up'][slot])
            if prompt:
                ckv_all, kr_all, k_pos = ckv, kr, pos
            else:
                ckv_all = jnp.concatenate([st['mla_latent'][slot].astype(ckv.dtype), ckv], axis=1)
                kr_all = jnp.concatenate([st['mla_krope'][slot].astype(kr.dtype), kr], axis=1)
                k_pos = jnp.arange(past_len + T, dtype=jnp.int32)
            mix = mla_attend(qn, qr, pos, ckv_all, kr_all, k_pos, p['mla_w_kv_up'][slot], p['mla_w_o'][slot])
            new['mla_latent'].append(ckv)
            new['mla_krope'].append(kr)
        else:
            q, k, v = swa_project(h, p['swa_w_in'][slot])
            if prompt:
                heads = swa_prompt(q, k, v, p['rel_bias'], p['swa_sinks'][slot])
                win = min(WINDOW, T)
                new_k, new_v = k[:, -win:], v[:, -win:]
            else:
                win = st['swa_k'].shape[2]
                k_all = jnp.concatenate([st['swa_k'][slot].astype(k.dtype), k], axis=1)
                v_all = jnp.concatenate([st['swa_v'][slot].astype(v.dtype), v], axis=1)
                k_pos = jnp.concatenate([past_len - win + jnp.arange(win, dtype=jnp.int32), pos])
                heads = swa_sample(q, pos, k_all, v_all, k_pos, p['rel_bias'], p['swa_sinks'][slot])
                new_k, new_v = k_all[:, -win:], v_all[:, -win:]
            mix = heads @ p['swa_w_o'][slot]
            new['swa_k'].append(new_k)
            new['swa_v'].append(new_v)
        x = x + g1 * mix
        h = rmsnorm(x, p['norm2'][layer]) * (1 + sc2) + sh2
        f, f_hist = conv_ffn(h, st['ffn_conv'][layer], p['ffn_w_in'][layer], p['ffn_conv_w'][layer],
                             p['ffn_conv_b'][layer], p['ffn_w_out'][layer])
        new['ffn_conv'].append(f_hist)
        x = x + g2 * f
    y = rmsnorm(x, p['final_norm'])
    return y, {name: jnp.stack(rows) for name, rows in new.items()}


def setup_inputs(seed: int = 0) -> dict:
    key = jax.random.key(seed)
    ks = iter(jax.random.split(key, 48))

    def nrm(shape, scale=1.0):
        return jax.random.normal(next(ks), shape, F32) * scale

    def gain(shape):
        return 1.0 + nrm(shape, 0.02)

    win = min(WINDOW, PAST_LEN)
    a_log = jnp.log(jax.random.uniform(next(ks), (N_GDN, GDN_HEADS), F32, minval=1.0, maxval=16.0))
    dt = jnp.exp(jax.random.uniform(next(ks), (N_GDN, GDN_HEADS), F32,
                                    minval=math.log(1e-3), maxval=math.log(1e-1)))
    dt_bias = dt + jnp.log(-jnp.expm1(-dt))
    return {
        'x_prompt': nrm((BATCH, SEQ, D_MODEL)),
        'x_sample': nrm((DEC_BATCH, DEC_SEQ, D_MODEL)),
        'c_prompt': nrm((BATCH, D_MODEL)),
        'c_sample': nrm((DEC_BATCH, D_MODEL)),
        'state_gdn_conv': nrm((N_GDN, DEC_BATCH, GDN_CONV - 1, GDN_QKV)),
        'state_gdn_S': nrm((N_GDN, DEC_BATCH, GDN_HEADS, GDN_DK, GDN_DV), 0.1),
        'cache_mla_latent': nrm((N_MLA, DEC_BATCH, PAST_LEN, MLA_KV_LORA)),
        'cache_mla_krope': nrm((N_MLA, DEC_BATCH, PAST_LEN, MLA_ROPE)),
        'cache_swa_k': nrm((N_SWA, DEC_BATCH, win, SWA_KV_HEADS, SWA_HD)),
        'cache_swa_v': nrm((N_SWA, DEC_BATCH, win, SWA_KV_HEADS, SWA_HD)),
        'state_ffn_conv': nrm((DEPTH, DEC_BATCH, FFN_CONV - 1, D_FF)),
        'ada_w': nrm((DEPTH, D_MODEL, 6 * D_MODEL), 0.02),
        'ada_b': nrm((DEPTH, 6 * D_MODEL), 0.02),
        'norm1': gain((DEPTH, D_MODEL)),
        'norm2': gain((DEPTH, D_MODEL)),
        'final_norm': gain((D_MODEL,)),
        'gdn_w_in': nrm((N_GDN, D_MODEL, GDN_IN), D_MODEL ** -0.5),
        'gdn_conv_w': nrm((N_GDN, GDN_CONV, GDN_QKV), GDN_CONV ** -0.5),
        'gdn_a_log': a_log,
        'gdn_dt_bias': dt_bias,
        'gdn_o_norm': gain((N_GDN, GDN_DV)),
        'gdn_w_o': nrm((N_GDN, GDN_HEADS * GDN_DV, D_MODEL), (GDN_HEADS * GDN_DV) ** -0.5),
        'mla_w_in': nrm((N_MLA, D_MODEL, MLA_IN), D_MODEL ** -0.5),
        'mla_q_norm': gain((N_MLA, MLA_Q_LORA)),
        'mla_kv_norm': gain((N_MLA, MLA_KV_LORA)),
        'mla_w_q_up': nrm((N_MLA, MLA_Q_LORA, MLA_HEADS * (MLA_NOPE + MLA_ROPE)), MLA_Q_LORA ** -0.5),
        'mla_w_kv_up': nrm((N_MLA, MLA_KV_LORA, MLA_HEADS * (MLA_NOPE + MLA_V)), MLA_KV_LORA ** -0.5),
        'mla_w_o': nrm((N_MLA, MLA_HEADS * MLA_V, D_MODEL), (MLA_HEADS * MLA_V) ** -0.5),
        'swa_w_in': nrm((N_SWA, D_MODEL, SWA_IN), D_MODEL ** -0.5),
        'swa_sinks': nrm((N_SWA, SWA_HEADS)),
        'swa_w_o': nrm((N_SWA, SWA_HEADS * SWA_HD, D_MODEL), (SWA_HEADS * SWA_HD) ** -0.5),
        'rel_bias': nrm((REL_BUCKETS, SWA_HEADS), 0.5),
        'ffn_w_in': nrm((DEPTH, D_MODEL, 2 * D_FF), D_MODEL ** -0.5),
        'ffn_conv_w': nrm((DEPTH, FFN_CONV, D_FF), FFN_CONV ** -0.5),
        'ffn_conv_b': nrm((DEPTH, D_FF), 0.02),
        'ffn_w_out': nrm((DEPTH, D_FF, D_MODEL), D_FF ** -0.5),
    }


def reference(x_prompt, x_sample, c_prompt, c_sample, state_gdn_conv, state_gdn_S, cache_mla_latent,
              cache_mla_krope, cache_swa_k, cache_swa_v, state_ffn_conv, ada_w, ada_b, norm1, norm2,
              final_norm, gdn_w_in, gdn_conv_w, gdn_a_log, gdn_dt_bias, gdn_o_norm, gdn_w_o, mla_w_in,
              mla_q_norm, mla_kv_norm, mla_w_q_up, mla_w_kv_up, mla_w_o, swa_w_in, swa_sinks, swa_w_o,
              rel_bias, ffn_w_in, ffn_conv_w, ffn_conv_b, ffn_w_out):
    p = dict(ada_w=ada_w, ada_b=ada_b, norm1=norm1, norm2=norm2, final_norm=final_norm,
             gdn_w_in=gdn_w_in, gdn_conv_w=gdn_conv_w, gdn_a_log=gdn_a_log, gdn_dt_bias=gdn_dt_bias,
             gdn_o_norm=gdn_o_norm, gdn_w_o=gdn_w_o, mla_w_in=mla_w_in, mla_q_norm=mla_q_norm,
             mla_kv_norm=mla_kv_norm, mla_w_q_up=mla_w_q_up, mla_w_kv_up=mla_w_kv_up, mla_w_o=mla_w_o,
             swa_w_in=swa_w_in, swa_sinks=swa_sinks, swa_w_o=swa_w_o, rel_bias=rel_bias,
             ffn_w_in=ffn_w_in, ffn_conv_w=ffn_conv_w, ffn_conv_b=ffn_conv_b, ffn_w_out=ffn_w_out)
    bp = x_prompt.shape[0]
    st_prompt = dict(gdn_conv=jnp.zeros((N_GDN, bp, GDN_CONV - 1, GDN_QKV), x_prompt.dtype),
                     gdn_S=jnp.zeros((N_GDN, bp, GDN_HEADS, GDN_DK, GDN_DV), F32),
                     ffn_conv=jnp.zeros((DEPTH, bp, FFN_CONV - 1, D_FF), x_prompt.dtype))
    st_sample = dict(gdn_conv=state_gdn_conv, gdn_S=state_gdn_S, mla_latent=cache_mla_latent,
                     mla_krope=cache_mla_krope, swa_k=cache_swa_k, swa_v=cache_swa_v,
                     ffn_conv=state_ffn_conv)
    past_len = cache_mla_latent.shape[2]
    y_prompt, sp = trunk(x_prompt, c_prompt, True, 0, st_prompt, p)
    y_sample, ss = trunk(x_sample, c_sample, False, past_len, st_sample, p)
    return (y_prompt, y_sample,
            sp['gdn_conv'], ss['gdn_conv'],
            sp['gdn_S'], ss['gdn_S'],
            sp['mla_latent'], ss['mla_latent'],
            sp['mla_krope'], ss['mla_krope'],
            sp['swa_k'], ss['swa_k'],
            sp['swa_v'], ss['swa_v'],
            sp['ffn_conv'], ss['ffn_conv'])
```

```python
import functools
import math

import jax
import jax.numpy as jnp
from jax import lax
from jax.experimental import pallas as pl
from jax.experimental.pallas import tpu as pltpu

F32 = jnp.float32
BF16 = jnp.bfloat16
HIGHEST = lax.Precision.HIGHEST

EPS = 1e-6
CHUNK = 64
D_MODEL = 1024

GDN_HEADS, GDN_DK, GDN_DV, GDN_CONV = 8, 128, 128, 4
GDN_QKV = GDN_HEADS * (2 * GDN_DK + GDN_DV)
GDN_CHUNK = 128

MLA_HEADS, MLA_Q_LORA, MLA_KV_LORA, MLA_NOPE, MLA_ROPE, MLA_V = 16, 384, 256, 64, 32, 64
ROPE_THETA = 10000.0
MLA_SLAB = 128

SWA_HEADS, SWA_KV_HEADS, SWA_HD, WINDOW = 16, 4, 64, 128
SWA_GROUP = SWA_HEADS // SWA_KV_HEADS
WIN_CHUNKS = WINDOW // CHUNK
REL_BUCKETS, REL_MAX_DIST = 32, 128

D_FF, FFN_CONV = 2816, 3

LANES = 128
CONV_PAD = 8
NEG = -1e30
VMEM_LIMIT = 56 * 1024 * 1024


def _cp(n_axes):
    return pltpu.CompilerParams(dimension_semantics=("arbitrary",) * n_axes, vmem_limit_bytes=VMEM_LIMIT)


def _silu(x):
    return x / (1.0 + jnp.exp(-x))


def _sigmoid(x):
    return 1.0 / (1.0 + jnp.exp(-x))


def _softplus(x):
    return jnp.maximum(x, 0.0) + jnp.log1p(jnp.exp(-jnp.abs(x)))


def _normmod(x, g, sc, sh):
    ms = jnp.mean(x * x, axis=-1, keepdims=True)
    return (x * lax.rsqrt(ms + EPS) * g) * (1.0 + sc) + sh


def _rms(x, g):
    ms = jnp.mean(x * x, axis=-1, keepdims=True)
    return x * lax.rsqrt(ms + EPS) * g


def _dot(a, b):
    return jnp.dot(a, b, preferred_element_type=F32)


def _dot_t(a, b):
    return lax.dot_general(a, b, (((1,), (1,)), ((), ())), preferred_element_type=F32)


def _const_spec(shape):
    n = len(shape)
    return pl.BlockSpec(shape, lambda *_: (0,) * n)


def _row_spec(tm, width, nt):
    return pl.BlockSpec((tm, width), lambda b, t: (b * nt + t, 0))


def _batch_vec_spec(width):
    return pl.BlockSpec((None, 1, width), lambda b, t: (b, 0, 0))


def _mod_kernel(c_ref, w_ref, b_ref, o_ref):
    cs = _silu(c_ref[...]).astype(BF16)
    o_ref[...] = _dot(cs, w_ref[...].astype(BF16)) + b_ref[...]


def _modulation(c_all, ada_w, ada_b):
    depth, d, n = ada_w.shape
    nb = c_all.shape[0]
    tn = 1536
    return pl.pallas_call(
        _mod_kernel,
        grid=(depth, n // tn),
        in_specs=[pl.BlockSpec((nb, d), lambda l, j: (0, 0)),
                  pl.BlockSpec((None, d, tn), lambda l, j: (l, 0, j)),
                  pl.BlockSpec((None, 1, tn), lambda l, j: (l, 0, j))],
        out_specs=pl.BlockSpec((None, nb, tn), lambda l, j: (l, 0, j)),
        out_shape=jax.ShapeDtypeStruct((depth, nb, n), F32),
        compiler_params=_cp(2),
        name="adaln_mod",
    )(c_all, ada_w, ada_b.reshape(depth, 1, n))


def _gdn_kernel(x_ref, sc_ref, sh_ref, gt_ref, ng_ref, hist_ref, s0_ref, wqkv_ref, wz_ref, wab_ref,
                cw_ref, alog_ref, dtb_ref, on_ref, wo_ref,
                xo_ref, nh_ref, so_ref,
                xp_scr, cv_scr, s_scr, o_scr, *, tm, rows):
    t = pl.program_id(1)
    nt = pl.num_programs(1)
    C = GDN_CHUNK
    K = GDN_CONV

    @pl.when(t == 0)
    def _():
        s_scr[...] = s0_ref[...]
        xp_scr[CONV_PAD - (K - 1):CONV_PAD, :] = hist_ref[...]

    x = x_ref[...]
    h = _normmod(x, ng_ref[...], sc_ref[...], sh_ref[...]).astype(BF16)

    qkv = _dot(h, wqkv_ref[...])
    xp_scr[CONV_PAD:CONV_PAD + tm, :] = qkv
    cw = cw_ref[...]
    conv = qkv * cw[K - 1:K]
    for j in range(K - 1):
        off = CONV_PAD - (K - 1) + j
        conv = conv + xp_scr[off:off + tm, :] * cw[j:j + 1]
    tail = qkv[tm - (K - 1):tm, :]
    xp_scr[CONV_PAD - (K - 1):CONV_PAD, :] = tail

    @pl.when(t == nt - 1)
    def _():
        nh_ref[...] = tail

    cv_scr[0:tm, :] = _silu(conv)
    if rows > tm:
        cv_scr[tm:rows, :] = jnp.zeros((rows - tm, GDN_QKV), F32)

    ab = _dot(h, wab_ref[...])
    g = -jnp.exp(alog_ref[...]) * _softplus(ab[:, :LANES] + dtb_ref[...])
    beta = _sigmoid(ab[:, LANES:])
    if rows > tm:
        zpad = jnp.zeros((rows - tm, LANES), F32)
        g = jnp.concatenate([g, zpad], axis=0)
        beta = jnp.concatenate([beta, zpad], axis=0)

    ri = lax.broadcasted_iota(jnp.int32, (rows, rows), 0)
    ci = lax.broadcasted_iota(jnp.int32, (rows, rows), 1)
    tril = jnp.where((ri >= ci) & ((ri // C) == (ci // C)), 1.0, 0.0).astype(F32)
    gc = jnp.dot(tril, g, precision=HIGHEST, preferred_element_type=F32)
    gct = gc.T

    ii = lax.broadcasted_iota(jnp.int32, (C, C), 0)
    jj = lax.broadcasted_iota(jnp.int32, (C, C), 1)
    same = []
    s = 8
    while s <= C:
        same.append((ii // s) == (jj // s))
        s *= 2
    scale = GDN_DK ** -0.5
    HK = GDN_HEADS * GDN_DK

    for hd in range(GDN_HEADS):
        ql = slice(hd * GDN_DK, (hd + 1) * GDN_DK)
        kl = slice(HK + hd * GDN_DK, HK + (hd + 1) * GDN_DK)
        vl = slice(2 * HK + hd * GDN_DV, 2 * HK + (hd + 1) * GDN_DV)
        S = s_scr[hd]
        for c in range(rows // C):
            rs = slice(c * C, (c + 1) * C)
            q = cv_scr[rs, ql]
            k = cv_scr[rs, kl]
            v = cv_scr[rs, vl]
            q = q * lax.rsqrt(jnp.sum(q * q, axis=-1, keepdims=True) + EPS) * scale
            k = k * lax.rsqrt(jnp.sum(k * k, axis=-1, keepdims=True) + EPS)
            gcol = gc[rs, hd:hd + 1]
            grow = gct[hd:hd + 1, rs]
            bcol = beta[rs, hd:hd + 1]
            decay = jnp.exp(jnp.where(ii >= jj, gcol - grow, NEG))
            kb = k * bcol
            kq = _dot_t(jnp.concatenate([kb, q], axis=0).astype(BF16), k.astype(BF16))
            A = jnp.where(ii > jj, kq[:C] * decay, 0.0)
            qk = kq[C:] * decay
            Ab = A.astype(BF16)
            D = jnp.where(same[0], Ab, 0)
            N = -jnp.where(same[0], A, 0.0)
            P = _dot(D, D)
            N = N + P + _dot(N.astype(BF16), P.astype(BF16))
            Pb = P.astype(BF16)
            P = _dot(Pb, Pb)
            N = N + P + _dot(N.astype(BF16), P.astype(BF16))
            for lvl in range(1, len(same)):
                off = jnp.where(same[lvl] & jnp.logical_not(same[lvl - 1]), Ab, 0)
                Nb = N.astype(BF16)
                X = off.astype(F32) + _dot(off, Nb)
                N = N - X - _dot(Nb, X.astype(BF16))
            eg = jnp.exp(gcol)
            rhs = jnp.concatenate([v * bcol, kb * eg], axis=1)
            sol = rhs + _dot(N.astype(BF16), rhs.astype(BF16))
            w_v, w_k = sol[:, :GDN_DV], sol[:, GDN_DV:]
            Sb = S.astype(BF16)
            u = w_v - _dot(w_k.astype(BF16), Sb)
            ub = u.astype(BF16)
            o = _dot((q * eg).astype(BF16), Sb) + _dot(qk.astype(BF16), ub)
            gl = gcol[C - 1:C, :]
            kdec = (k * jnp.exp(gl - gcol)).T.astype(BF16)
            S = S * jnp.exp(gl) + _dot(kdec, ub)
            o = _rms(o, on_ref[...])
            o_scr[rs, hd * GDN_DV:(hd + 1) * GDN_DV] = o
        s_scr[hd] = S

    z = _dot(h, wz_ref[...])
    og = (o_scr[0:tm, :] * _silu(z)).astype(BF16)
    xo_ref[...] = x + gt_ref[...] * _dot(og, wo_ref[...])

    @pl.when(t == nt - 1)
    def _():
        so_ref[...] = s_scr[...]


def _gdn_layer(x2d, nb, T, sc, sh, gt, ng, hist, s0, w_in, conv_w, a_log, dt_bias, o_norm, w_o, tm):
    nt = T // tm
    rows = max(tm, GDN_CHUNK)
    H, dv = GDN_HEADS, GDN_DV
    wqkv = w_in[:, :GDN_QKV].astype(BF16)
    wz = w_in[:, GDN_QKV:GDN_QKV + H * dv].astype(BF16)
    wa = w_in[:, GDN_QKV + H * dv:GDN_QKV + H * dv + H]
    wb = w_in[:, GDN_QKV + H * dv + H:]
    lane_pad = ((0, 0), (0, LANES - H))
    wab = jnp.concatenate([jnp.pad(wa, lane_pad), jnp.pad(wb, lane_pad)], axis=1).astype(BF16)
    alog = jnp.pad(a_log.reshape(1, H), lane_pad)
    dtb = jnp.pad(dt_bias.reshape(1, H), lane_pad)
    kern = functools.partial(_gdn_kernel, tm=tm, rows=rows)
    return pl.pallas_call(
        kern,
        grid=(nb, nt),
        in_specs=[_row_spec(tm, D_MODEL, nt), _batch_vec_spec(D_MODEL), _batch_vec_spec(D_MODEL),
                  _batch_vec_spec(D_MODEL), _const_spec((1, D_MODEL)),
                  pl.BlockSpec((None, GDN_CONV - 1, GDN_QKV), lambda b, t: (b, 0, 0)),
                  pl.BlockSpec((None, H, GDN_DK, dv), lambda b, t: (b, 0, 0, 0)),
                  _const_spec((D_MODEL, GDN_QKV)), _const_spec((D_MODEL, H * dv)),
                  _const_spec((D_MODEL, 2 * LANES)), _const_spec((GDN_CONV, GDN_QKV)),
                  _const_spec((1, LANES)), _const_spec((1, LANES)), _const_spec((1, dv)),
                  _const_spec((H * dv, D_MODEL))],
        out_specs=[_row_spec(tm, D_MODEL, nt),
                   pl.BlockSpec((None, GDN_CONV - 1, GDN_QKV), lambda b, t: (b, 0, 0)),
                   pl.BlockSpec((None, H, GDN_DK, dv), lambda b, t: (b, 0, 0, 0))],
        out_shape=[jax.ShapeDtypeStruct((nb * T, D_MODEL), F32),
                   jax.ShapeDtypeStruct((nb, GDN_CONV - 1, GDN_QKV), F32),
                   jax.ShapeDtypeStruct((nb, H, GDN_DK, dv), F32)],
        scratch_shapes=[pltpu.VMEM((CONV_PAD + tm, GDN_QKV), F32),
                        pltpu.VMEM((rows, GDN_QKV), F32),
                        pltpu.VMEM((H, GDN_DK, dv), F32),
                        pltpu.VMEM((rows, H * dv), F32)],
        compiler_params=_cp(2),
        name="gdn_layer",
    )(x2d, sc, sh, gt, ng.reshape(1, D_MODEL), hist, s0, wqkv, wz, wab, conv_w, alog, dtb,
      o_norm.reshape(1, dv), w_o.astype(BF16))


def _ffn_kernel(x_ref, sc_ref, sh_ref, gt_ref, ng_ref, hist_ref, wg_ref, wu_ref, cw_ref, cb_ref, wo_ref, fn_ref,
                xo_ref, nh_ref, xp_scr, *, tm, final):
    t = pl.program_id(1)
    nt = pl.num_programs(1)
    K = FFN_CONV

    @pl.when(t == 0)
    def _():
        xp_scr[CONV_PAD - (K - 1):CONV_PAD, :] = hist_ref[...]

    x = x_ref[...]
    h = _normmod(x, ng_ref[...], sc_ref[...], sh_ref[...]).astype(BF16)
    gate = _dot(h, wg_ref[...])
    xp_scr[CONV_PAD:CONV_PAD + tm, :] = gate
    cw = cw_ref[...]
    conv = gate * cw[K - 1:K]
    for j in range(K - 1):
        off = CONV_PAD - (K - 1) + j
        conv = conv + xp_scr[off:off + tm, :] * cw[j:j + 1]
    tail = gate[tm - (K - 1):tm, :]
    xp_scr[CONV_PAD - (K - 1):CONV_PAD, :] = tail

    @pl.when(t == nt - 1)
    def _():
        nh_ref[...] = tail

    up = _dot(h, wu_ref[...])
    act = (_silu(conv + cb_ref[...]) * up).astype(BF16)
    y = x + gt_ref[...] * _dot(act, wo_ref[...])
    if final:
        y = _rms(y, fn_ref[...])
    xo_ref[...] = y


def _ffn_layer(x2d, nb, T, sc, sh, gt, ng, hist, w_in, conv_w, conv_b, w_out, final_g, final, tm):
    nt = T // tm
    wg = w_in[:, :D_FF].astype(BF16)
    wu = w_in[:, D_FF:].astype(BF16)
    kern = functools.partial(_ffn_kernel, tm=tm, final=final)
    return pl.pallas_call(
        kern,
        grid=(nb, nt),
        in_specs=[_row_spec(tm, D_MODEL, nt), _batch_vec_spec(D_MODEL), _batch_vec_spec(D_MODEL),
                  _batch_vec_spec(D_MODEL), _const_spec((1, D_MODEL)),
                  pl.BlockSpec((None, FFN_CONV - 1, D_FF), lambda b, t: (b, 0, 0)),
                  _const_spec((D_MODEL, D_FF)), _const_spec((D_MODEL, D_FF)),
                  _const_spec((FFN_CONV, D_FF)), _const_spec((1, D_FF)),
                  _const_spec((D_FF, D_MODEL)), _const_spec((1, D_MODEL))],
        out_specs=[_row_spec(tm, D_MODEL, nt),
                   pl.BlockSpec((None, FFN_CONV - 1, D_FF), lambda b, t: (b, 0, 0))],
        out_shape=[jax.ShapeDtypeStruct((nb * T, D_MODEL), F32),
                   jax.ShapeDtypeStruct((nb, FFN_CONV - 1, D_FF), F32)],
        scratch_shapes=[pltpu.VMEM((CONV_PAD + tm, D_FF), F32)],
        compiler_params=_cp(2),
        name="conv_ffn_layer",
    )(x2d, sc, sh, gt, ng.reshape(1, D_MODEL), hist, wg, wu, conv_w, conv_b.reshape(1, D_FF),
      w_out.astype(BF16), final_g.reshape(1, D_MODEL))


def _proj_res_kernel(a_ref, w_ref, x_ref, gt_ref, o_ref):
    o_ref[...] = x_ref[...] + gt_ref[...] * _dot(a_ref[...], w_ref[...])


def _proj_res(a2d, w, x2d, gt, nb, T, tm):
    nt = T // tm
    kdim = a2d.shape[1]
    return pl.pallas_call(
        _proj_res_kernel,
        grid=(nb, nt),
        in_specs=[_row_spec(tm, kdim, nt), _const_spec((kdim, D_MODEL)), _row_spec(tm, D_MODEL, nt),
                  _batch_vec_spec(D_MODEL)],
        out_specs=_row_spec(tm, D_MODEL, nt),
        out_shape=jax.ShapeDtypeStruct((nb * T, D_MODEL), F32),
        compiler_params=_cp(2),
        name="attn_out_proj",
    )(a2d, w.astype(BF16), x2d, gt)


def _mla_q_kernel(x_ref, sc_ref, sh_ref, ng_ref, wq_ref, wkv_ref, wr_ref, qn_ref, kvn_ref, wu1_ref, wu2_ref,
                  cos_ref, sin_ref, q_ref, ckv_ref, kr_ref):
    h = _normmod(x_ref[...], ng_ref[...], sc_ref[...], sh_ref[...]).astype(BF16)
    cq = _rms(_dot(h, wq_ref[...]), qn_ref[...]).astype(BF16)
    ckv_ref[...] = _rms(_dot(h, wkv_ref[...]), kvn_ref[...])
    cs = cos_ref[...]
    sn = sin_ref[...]
    r = _dot(h, wr_ref[...])
    kr_ref[...] = r[:, :MLA_SLAB] * cs + r[:, MLA_SLAB:] * sn
    q1 = _dot(cq, wu1_ref[...])
    q2 = _dot(cq, wu2_ref[...])
    for hd in range(MLA_HEADS):
        sl = slice(hd * MLA_SLAB, (hd + 1) * MLA_SLAB)
        q_ref[:, sl] = (q1[:, sl] * cs + q2[:, sl] * sn).astype(BF16)


def _rope_tables(pos):
    half = MLA_ROPE // 2
    inv = ROPE_THETA ** (-jnp.arange(half, dtype=F32) / half)
    ang = pos.astype(F32)[:, None] * inv[None, :]
    cos, sin = jnp.cos(ang), jnp.sin(ang)
    n = pos.shape[0]
    pad = jnp.zeros((n, MLA_SLAB - MLA_NOPE - MLA_ROPE), F32)
    cs = jnp.concatenate([jnp.ones((n, MLA_NOPE), F32), cos, cos, pad], axis=1)
    sn = jnp.concatenate([jnp.zeros((n, MLA_NOPE), F32), -sin, sin, pad], axis=1)
    return cs, sn


def _slab_weights(w_q_up, w_kv_up, w_r):
    half = MLA_ROPE // 2
    hq = MLA_NOPE + MLA_ROPE
    padw = MLA_SLAB - hq
    wq = w_q_up.reshape(MLA_Q_LORA, MLA_HEADS, hq)
    wq_n, wq_r = wq[..., :MLA_NOPE], wq[..., MLA_NOPE:]
    wq_sw = jnp.concatenate([wq_r[..., half:], wq_r[..., :half]], axis=-1)
    zq = jnp.zeros((MLA_Q_LORA, MLA_HEADS, padw), F32)
    wu1 = jnp.concatenate([wq_n, wq_r, zq], axis=-1).reshape(MLA_Q_LORA, MLA_HEADS * MLA_SLAB)
    wu2 = jnp.concatenate([jnp.zeros_like(wq_n), wq_sw, zq], axis=-1).reshape(MLA_Q_LORA, MLA_HEADS * MLA_SLAB)
    wkv = w_kv_up.reshape(MLA_KV_LORA, MLA_HEADS, MLA_NOPE + MLA_V)
    wk = jnp.concatenate([wkv[..., :MLA_NOPE], jnp.zeros((MLA_KV_LORA, MLA_HEADS, MLA_SLAB - MLA_NOPE), F32)],
                         axis=-1).reshape(MLA_KV_LORA, MLA_HEADS * MLA_SLAB)
    wv = wkv[..., MLA_NOPE:].reshape(MLA_KV_LORA, MLA_HEADS * MLA_V)
    d = w_r.shape[0]
    wr_sw = jnp.concatenate([w_r[:, half:], w_r[:, :half]], axis=1)
    zl = jnp.zeros((d, MLA_NOPE), F32)
    zr = jnp.zeros((d, padw), F32)
    wr2 = jnp.concatenate([zl, w_r, zr, zl, wr_sw, zr], axis=1)
    return wu1.astype(BF16), wu2.astype(BF16), wk.astype(BF16), wv.astype(BF16), wr2.astype(BF16)


def _mla_q(x2d, nb, T, sc, sh, ng, wq, wkv, wr2, qn, kvn, wu1, wu2, cs, sn, tm):
    nt = T // tm
    qw = MLA_HEADS * MLA_SLAB
    return pl.pallas_call(
        _mla_q_kernel,
        grid=(nb, nt),
        in_specs=[_row_spec(tm, D_MODEL, nt), _batch_vec_spec(D_MODEL), _batch_vec_spec(D_MODEL),
                  _const_spec((1, D_MODEL)), _const_spec((D_MODEL, MLA_Q_LORA)),
                  _const_spec((D_MODEL, MLA_KV_LORA)), _const_spec((D_MODEL, 2 * MLA_SLAB)),
                  _const_spec((1, MLA_Q_LORA)), _const_spec((1, MLA_KV_LORA)),
                  _const_spec((MLA_Q_LORA, qw)), _const_spec((MLA_Q_LORA, qw)),
                  pl.BlockSpec((tm, MLA_SLAB), lambda b, t: (t, 0)),
                  pl.BlockSpec((tm, MLA_SLAB), lambda b, t: (t, 0))],
        out_specs=[_row_spec(tm, qw, nt), _row_spec(tm, MLA_KV_LORA, nt), _row_spec(tm, MLA_SLAB, nt)],
        out_shape=[jax.ShapeDtypeStruct((nb * T, qw), BF16),
                   jax.ShapeDtypeStruct((nb * T, MLA_KV_LORA), F32),
                   jax.ShapeDtypeStruct((nb * T, MLA_SLAB), F32)],
        compiler_params=_cp(2),
        name="mla_q_proj",
    )(x2d, sc, sh, ng.reshape(1, D_MODEL), wq, wkv, wr2, qn.reshape(1, -1), kvn.reshape(1, -1), wu1, wu2, cs, sn)


def _mla_kv_kernel(ckv_ref, kr_ref, wk_ref, wv_ref, k_ref, v_ref):
    c = ckv_ref[...].astype(BF16)
    kn = _dot(c, wk_ref[...])
    kr = kr_ref[...]
    for hd in range(MLA_HEADS):
        sl = slice(hd * MLA_SLAB, (hd + 1) * MLA_SLAB)
        k_ref[:, sl] = (kn[:, sl] + kr).astype(BF16)
    v_ref[...] = _dot(c, wv_ref[...]).astype(BF16)


def _mla_kv(ckv2d, kr2d, wk, wv, tm):
    rows = ckv2d.shape[0]
    kw = MLA_HEADS * MLA_SLAB
    vw = MLA_HEADS * MLA_V
    return pl.pallas_call(
        _mla_kv_kernel,
        grid=(rows // tm,),
        in_specs=[pl.BlockSpec((tm, MLA_KV_LORA), lambda i: (i, 0)), pl.BlockSpec((tm, MLA_SLAB), lambda i: (i, 0)),
                  _const_spec((MLA_KV_LORA, kw)), _const_spec((MLA_KV_LORA, vw))],
        out_specs=[pl.BlockSpec((tm, kw), lambda i: (i, 0)), pl.BlockSpec((tm, vw), lambda i: (i, 0))],
        out_shape=[jax.ShapeDtypeStruct((rows, kw), BF16), jax.ShapeDtypeStruct((rows, vw), BF16)],
        compiler_params=_cp(1),
        name="mla_kv_up",
    )(ckv2d, kr2d, wk, wv)


def _mla_attn_kernel(q_ref, k_ref, v_ref, o_ref, *, tq, tk, causal, n_kv_full):
    i = pl.program_id(2)
    scale = (MLA_NOPE + MLA_ROPE) ** -0.5
    q = q_ref[...]
    qs = [q[:, :MLA_SLAB], q[:, MLA_SLAB:]]
    n_kv = (i + 1) if causal else n_kv_full

    def body(j, carry):
        off = pl.multiple_of(j * tk, tk)
        kblk = k_ref[pl.ds(off, tk), :]
        vblk = v_ref[pl.ds(off, tk), :]
        if causal:
            row = lax.broadcasted_iota(jnp.int32, (tq, tk), 0) + i * tq
            col = lax.broadcasted_iota(jnp.int32, (tq, tk), 1) + off
            keep = (col // CHUNK) <= (row // CHUNK)
        out = []
        for hh in range(2):
            m, l, acc = carry[hh]
            s = _dot_t(qs[hh], kblk[:, hh * MLA_SLAB:(hh + 1) * MLA_SLAB]) * scale
            if causal:
                s = jnp.where(keep, s, NEG)
            m_new = jnp.maximum(m, jnp.max(s, axis=-1, keepdims=True))
            p = jnp.exp(s - m_new)
            alpha = jnp.exp(m - m_new)
            l = alpha * l + jnp.sum(p, axis=-1, keepdims=True)
            acc = alpha * acc + _dot(p.astype(BF16), vblk)
            out.append((m_new, l, acc))
        return tuple(out)

    init = tuple((jnp.full((tq, 1), NEG, F32), jnp.zeros((tq, 1), F32), jnp.zeros((tq, 2 * MLA_V), F32))
                 for _ in range(2))
    res = lax.fori_loop(0, n_kv, body, init)
    o0 = res[0][2] / res[0][1]
    o1 = res[1][2] / res[1][1]
    lane = lax.broadcasted_iota(jnp.int32, (tq, 2 * MLA_V), 1)
    o_ref[...] = jnp.where(lane < MLA_V, o0, o1).astype(BF16)


def _mla_attn(q3, k3, v3, tq, tk, causal):
    nb, Tq, _ = q3.shape
    Tk = k3.shape[1]
    npairs = MLA_HEADS // 2
    kern = functools.partial(_mla_attn_kernel, tq=tq, tk=tk, causal=causal, n_kv_full=Tk // tk)
    return pl.pallas_call(
        kern,
        grid=(nb, npairs, Tq // tq),
        in_specs=[pl.BlockSpec((None, tq, 2 * MLA_SLAB), lambda b, p, i: (b, i, p)),
                  pl.BlockSpec((None, Tk, 2 * MLA_SLAB), lambda b, p, i: (b, 0, p)),
                  pl.BlockSpec((None, Tk, 2 * MLA_V), lambda b, p, i: (b, 0, p))],
        out_specs=pl.BlockSpec((None, tq, 2 * MLA_V), lambda b, p, i: (b, i, p)),
        out_shape=jax.ShapeDtypeStruct((nb, Tq, MLA_HEADS * MLA_V), BF16),
        compiler_params=_cp(3),
        name="mla_attention",
    )(q3, k3, v3)


def _swa_proj_kernel(x_ref, sc_ref, sh_ref, ng_ref, w_ref, q_ref, kv_ref, last_ref, *, tm, keep):
    h = _normmod(x_ref[...], ng_ref[...], sc_ref[...], sh_ref[...]).astype(BF16)
    proj = _dot(h, w_ref[...])
    nq = SWA_HEADS * SWA_HD
    q_ref[...] = proj[:, :nq].astype(BF16)
    kv = proj[:, nq:]
    kv_ref[...] = kv.astype(BF16)
    last_ref[...] = kv[tm - keep:, :]


def _swa_proj(x2d, nb, T, sc, sh, ng, w_in, tm):
    nt = T // tm
    keep = min(WINDOW, T)
    nq = SWA_HEADS * SWA_HD
    nkv = 2 * SWA_KV_HEADS * SWA_HD
    kern = functools.partial(_swa_proj_kernel, tm=tm, keep=keep)
    return pl.pallas_call(
        kern,
        grid=(nb, nt),
        in_specs=[_row_spec(tm, D_MODEL, nt), _batch_vec_spec(D_MODEL), _batch_vec_spec(D_MODEL),
                  _const_spec((1, D_MODEL)), _const_spec((D_MODEL, nq + nkv))],
        out_specs=[_row_spec(tm, nq, nt), _row_spec(tm, nkv, nt),
                   pl.BlockSpec((None, keep, nkv), lambda b, t: (b, 0, 0))],
        out_shape=[jax.ShapeDtypeStruct((nb * T, nq), BF16), jax.ShapeDtypeStruct((nb * T, nkv), BF16),
                   jax.ShapeDtypeStruct((nb, keep, nkv), F32)],
        compiler_params=_cp(2),
        name="swa_proj",
    )(x2d, sc, sh, ng.reshape(1, D_MODEL), w_in.astype(BF16))


def _t5_bias_kernel(bucket_ref, table_ref, o_ref):
    hd = pl.program_id(0)
    bucket = bucket_ref[...]
    acc = jnp.full(bucket.shape, NEG, F32)
    for b in range(REL_BUCKETS):
        acc = jnp.where(bucket == b, table_ref[b, hd], acc)
    o_ref[...] = acc


def _t5_bias(rel_bias, q_pos, k_pos, valid):
    n = q_pos[:, None] - k_pos[None, :]
    half = REL_BUCKETS // 2
    exact = half // 2
    side = jnp.where(n < 0, half, 0)
    n = jnp.abs(n)
    log_b = exact + (jnp.log(jnp.maximum(n, 1).astype(F32) / exact)
                     / math.log(REL_MAX_DIST / exact) * (half - exact)).astype(jnp.int32)
    bucket = side + jnp.where(n < exact, n, jnp.minimum(log_b, half - 1))
    bucket = jnp.where(valid, bucket, -1).astype(jnp.int32)
    tq, tk = bucket.shape
    return pl.pallas_call(
        _t5_bias_kernel,
        grid=(SWA_HEADS,),
        in_specs=[pl.BlockSpec((tq, tk), lambda h: (0, 0)),
                  pl.BlockSpec(memory_space=pltpu.SMEM)],
        out_specs=pl.BlockSpec((None, tq, tk), lambda h: (h, 0, 0)),
        out_shape=jax.ShapeDtypeStruct((SWA_HEADS, tq, tk), F32),
        compiler_params=_cp(1),
        name="t5_bias",
    )(bucket, rel_bias)


def _swa_attn_kernel(q_ref, kv_ref, bias_ref, sink_ref, o_ref, *, tq, lk, npad):
    i = pl.program_id(1)
    off = pl.multiple_of(i * tq, tq)
    kvw = kv_ref[pl.ds(off, lk), :]
    nk = SWA_KV_HEADS * SWA_HD
    q = q_ref[...]
    if npad:
        col = lax.broadcasted_iota(jnp.int32, (tq, lk), 1) + off
        real = col >= npad
    outs = []
    for hd in range(SWA_HEADS):
        kvh = hd // SWA_GROUP
        qh = q[:, hd * SWA_HD:(hd + 1) * SWA_HD]
        kh = kvw[:, kvh * SWA_HD:(kvh + 1) * SWA_HD]
        vh = kvw[:, nk + kvh * SWA_HD:nk + (kvh + 1) * SWA_HD]
        s = _dot_t(qh, kh) * (SWA_HD ** -0.5) + bias_ref[hd]
        if npad:
            s = jnp.where(real, s, NEG)
        sink = sink_ref[hd]
        m = jnp.maximum(jnp.max(s, axis=-1, keepdims=True), sink)
        p = jnp.exp(s - m)
        p = p / (jnp.sum(p, axis=-1, keepdims=True) + jnp.exp(sink - m))
        outs.append(_dot(p.astype(BF16), vh))
    o_ref[...] = jnp.concatenate(outs, axis=-1).astype(BF16)


def _swa_attn(q3, kv3, bias, sinks, tq, lk, npad):
    nb, Tq, nq = q3.shape
    Tk, nkv = kv3.shape[1:]
    kern = functools.partial(_swa_attn_kernel, tq=tq, lk=lk, npad=npad)
    return pl.pallas_call(
        kern,
        grid=(nb, Tq // tq),
        in_specs=[pl.BlockSpec((None, tq, nq), lambda b, i: (b, i, 0)),
                  pl.BlockSpec((None, Tk, nkv), lambda b, i: (b, 0, 0)),
                  _const_spec((SWA_HEADS, tq, lk)),
                  pl.BlockSpec(memory_space=pltpu.SMEM)],
        out_specs=pl.BlockSpec((None, tq, nq), lambda b, i: (b, i, 0)),
        out_shape=jax.ShapeDtypeStruct((nb, Tq, nq), BF16),
        compiler_params=_cp(2),
        name="swa_attention",
    )(q3, kv3, bias, sinks)


def _trunk(x, mods, prompt, past_len, st, p, tm, ffn_tm):
    B, T, _ = x.shape
    x2d = x.reshape(B * T, D_MODEL)
    pos = jnp.arange(T, dtype=jnp.int32) + (0 if prompt else past_len)
    new = dict(gdn_conv=[], gdn_S=[], mla_latent=[], mla_krope=[], swa_k=[], swa_v=[], ffn_conv=[])
    depth = p['ada_w'].shape[0]
    for layer in range(depth):
        kind, slot = layer % 3, layer // 3
        sh1, sc1, g1, sh2, sc2, g2 = [mods[layer, :, j * D_MODEL:(j + 1) * D_MODEL][:, None, :] for j in range(6)]
        ng = p['norm1'][layer]
        if kind == 0:
            x2d, conv_h, S = _gdn_layer(x2d, B, T, sc1, sh1, g1, ng, st['gdn_conv'][slot], st['gdn_S'][slot],
                                        p['gdn_w_in'][slot], p['gdn_conv_w'][slot], p['gdn_a_log'][slot],
                                        p['gdn_dt_bias'][slot], p['gdn_o_norm'][slot], p['gdn_w_o'][slot], tm)
            new['gdn_conv'].append(conv_h)
            new['gdn_S'].append(S)
        elif kind == 1:
            w_in = p['mla_w_in'][slot]
            wq = w_in[:, :MLA_Q_LORA].astype(BF16)
            wkv = w_in[:, MLA_Q_LORA:MLA_Q_LORA + MLA_KV_LORA].astype(BF16)
            wu1, wu2, wk, wv, wr2 = _slab_weights(p['mla_w_q_up'][slot], p['mla_w_kv_up'][slot],
                                                  w_in[:, MLA_Q_LORA + MLA_KV_LORA:])
            cs, sn = _rope_tables(pos)
            qcat, ckv, kr = _mla_q(x2d, B, T, sc1, sh1, ng, wq, wkv, wr2, p['mla_q_norm'][slot],
                                   p['mla_kv_norm'][slot], wu1, wu2, cs, sn, tm)
            ckv3 = ckv.reshape(B, T, MLA_KV_LORA)
            kr3 = kr.reshape(B, T, MLA_SLAB)
            if prompt:
                ckv_all, kr_all, Tk = ckv, kr, T
                kv_tm, tq, tk = tm, tm, tm
            else:
                assert past_len % CHUNK + T <= CHUNK
                cache_kr = jnp.pad(st['mla_krope'][slot].astype(F32),
                                   ((0, 0), (0, 0), (MLA_NOPE, MLA_SLAB - MLA_NOPE - MLA_ROPE)))
                Tk = past_len + T
                ckv_all = jnp.concatenate([st['mla_latent'][slot].astype(F32), ckv3], axis=1).reshape(B * Tk, -1)
                kr_all = jnp.concatenate([cache_kr, kr3], axis=1).reshape(B * Tk, MLA_SLAB)
                kv_tm, tq, tk = Tk, T, Tk
            kcat, vv = _mla_kv(ckv_all, kr_all, wk, wv, kv_tm)
            o = _mla_attn(qcat.reshape(B, T, -1), kcat.reshape(B, Tk, -1), vv.reshape(B, Tk, -1), tq, tk, prompt)
            x2d = _proj_res(o.reshape(B * T, -1), p['mla_w_o'][slot], x2d, g1, B, T, tm)
            new['mla_latent'].append(ckv3)
            new['mla_krope'].append(kr3[:, :, MLA_NOPE:MLA_NOPE + MLA_ROPE])
        else:
            q, kvb, kv_last = _swa_proj(x2d, B, T, sc1, sh1, ng, p['swa_w_in'][slot], tm)
            nk = SWA_KV_HEADS * SWA_HD
            if prompt:
                tq = 2 * CHUNK
                lk = tq + WINDOW
                rq = WINDOW + jnp.arange(tq)
                rk = jnp.arange(lk)
                d = rq[:, None] // CHUNK - rk[None, :] // CHUNK
                bias = _t5_bias(p['rel_bias'], rq, rk, (d >= 0) & (d <= WIN_CHUNKS))
                kv3 = jnp.pad(kvb.reshape(B, T, 2 * nk), ((0, 0), (WINDOW, 0), (0, 0)))
                heads = _swa_attn(q.reshape(B, T, -1), kv3, bias, p['swa_sinks'][slot], tq, lk, WINDOW)
                new_k, new_v = kv_last[:, :, :nk], kv_last[:, :, nk:]
            else:
                win = st['swa_k'].shape[2]
                ck = st['swa_k'][slot].reshape(B, win, nk).astype(F32)
                cv = st['swa_v'][slot].reshape(B, win, nk).astype(F32)
                k_all = jnp.concatenate([ck, kv_last[:, :, :nk]], axis=1)
                v_all = jnp.concatenate([cv, kv_last[:, :, nk:]], axis=1)
                k_pos = jnp.concatenate([past_len - win + jnp.arange(win, dtype=jnp.int32), pos])
                d = (pos // CHUNK)[:, None] - (k_pos // CHUNK)[None, :]
                bias = _t5_bias(p['rel_bias'], pos, k_pos, (d >= 0) & (d <= WIN_CHUNKS))
                kv3 = jnp.concatenate([k_all, v_all], axis=-1).astype(BF16)
                heads = _swa_attn(q.reshape(B, T, -1), kv3, bias, p['swa_sinks'][slot], T, win + T, 0)
                new_k, new_v = k_all[:, -win:], v_all[:, -win:]
            x2d = _proj_res(heads.reshape(B * T, -1), p['swa_w_o'][slot], x2d, g1, B, T, tm)
            new['swa_k'].append(new_k.reshape(B, -1, SWA_KV_HEADS, SWA_HD))
            new['swa_v'].append(new_v.reshape(B, -1, SWA_KV_HEADS, SWA_HD))
        x2d, f_hist = _ffn_layer(x2d, B, T, sc2, sh2, g2, p['norm2'][layer], st['ffn_conv'][layer],
                                 p['ffn_w_in'][layer], p['ffn_conv_w'][layer], p['ffn_conv_b'][layer],
                                 p['ffn_w_out'][layer], p['final_norm'], layer == depth - 1, ffn_tm)
        new['ffn_conv'].append(f_hist)
    return x2d.reshape(B, T, D_MODEL), {name: jnp.stack(rows) for name, rows in new.items()}


def kernel(x_prompt, x_sample, c_prompt, c_sample, state_gdn_conv, state_gdn_S, cache_mla_latent, cache_mla_krope, cache_swa_k, cache_swa_v, state_ffn_conv, ada_w, ada_b, norm1, norm2, final_norm, gdn_w_in, gdn_conv_w, gdn_a_log, gdn_dt_bias, gdn_o_norm, gdn_w_o, mla_w_in, mla_q_norm, mla_kv_norm, mla_w_q_up, mla_w_kv_up, mla_w_o, swa_w_in, swa_sinks, swa_w_o, rel_bias, ffn_w_in, ffn_conv_w, ffn_conv_b, ffn_w_out):
    p = dict(ada_w=ada_w, ada_b=ada_b, norm1=norm1, norm2=norm2, final_norm=final_norm,
             gdn_w_in=gdn_w_in, gdn_conv_w=gdn_conv_w, gdn_a_log=gdn_a_log, gdn_dt_bias=gdn_dt_bias,
             gdn_o_norm=gdn_o_norm, gdn_w_o=gdn_w_o, mla_w_in=mla_w_in, mla_q_norm=mla_q_norm,
             mla_kv_norm=mla_kv_norm, mla_w_q_up=mla_w_q_up, mla_w_kv_up=mla_w_kv_up, mla_w_o=mla_w_o,
             swa_w_in=swa_w_in, swa_sinks=swa_sinks, swa_w_o=swa_w_o, rel_bias=rel_bias,
             ffn_w_in=ffn_w_in, ffn_conv_w=ffn_conv_w, ffn_conv_b=ffn_conv_b, ffn_w_out=ffn_w_out)
    bp, tp = x_prompt.shape[:2]
    bs, ts = x_sample.shape[:2]
    n_gdn, n_ffn = state_gdn_conv.shape[0], state_ffn_conv.shape[0]
    st_prompt = dict(gdn_conv=jnp.zeros((n_gdn, bp, GDN_CONV - 1, GDN_QKV), F32),
                     gdn_S=jnp.zeros((n_gdn, bp, GDN_HEADS, GDN_DK, GDN_DV), F32),
                     ffn_conv=jnp.zeros((n_ffn, bp, FFN_CONV - 1, D_FF), F32))
    st_sample = dict(gdn_conv=state_gdn_conv, gdn_S=state_gdn_S, mla_latent=cache_mla_latent,
                     mla_krope=cache_mla_krope, swa_k=cache_swa_k, swa_v=cache_swa_v,
                     ffn_conv=state_ffn_conv)
    past_len = cache_mla_latent.shape[2]
    c_all = jnp.concatenate([c_prompt, c_sample], axis=0)
    mods = _modulation(c_all, ada_w, ada_b)
    y_prompt, sp = _trunk(x_prompt, mods[:, :bp], True, 0, st_prompt, p, min(256, tp), min(256, tp))
    y_sample, ss = _trunk(x_sample, mods[:, bp:bp + bs], False, past_len, st_sample, p, ts, ts)
    return (y_prompt, y_sample,
            sp['gdn_conv'], ss['gdn_conv'],
            sp['gdn_S'], ss['gdn_S'],
            sp['mla_latent'], ss['mla_latent'],
            sp['mla_krope'], ss['mla_krope'],
            sp['swa_k'], ss['swa_k'],
            sp['swa_v'], ss['swa_v'],
            sp['ffn_conv'], ss['ffn_conv'])
```

```python
import functools
import math

import jax
import jax.numpy as jnp
from jax import lax
from jax.experimental import pallas as pl
from jax.experimental.pallas import tpu as pltpu

F32 = jnp.float32
BF16 = jnp.bfloat16
HIGHEST = lax.Precision.HIGHEST

EPS = 1e-6
CHUNK = 64
D_MODEL = 1024

GDN_HEADS, GDN_DK, GDN_DV, GDN_CONV = 8, 128, 128, 4
GDN_QKV = GDN_HEADS * (2 * GDN_DK + GDN_DV)
GDN_CHUNK = 128

MLA_HEADS, MLA_Q_LORA, MLA_KV_LORA, MLA_NOPE, MLA_ROPE, MLA_V = 16, 384, 256, 64, 32, 64
ROPE_THETA = 10000.0
MLA_SLAB = 128

SWA_HEADS, SWA_KV_HEADS, SWA_HD, WINDOW = 16, 4, 64, 128
SWA_GROUP = SWA_HEADS // SWA_KV_HEADS
WIN_CHUNKS = WINDOW // CHUNK
REL_BUCKETS, REL_MAX_DIST = 32, 128

D_FF, FFN_CONV = 2816, 3

LANES = 128
CONV_PAD = 8
NEG = -1e30
VMEM_LIMIT = 56 * 1024 * 1024


def _cp(n_axes):
    return pltpu.CompilerParams(dimension_semantics=("arbitrary",) * n_axes, vmem_limit_bytes=VMEM_LIMIT)


def _silu(x):
    return x / (1.0 + jnp.exp(-x))


def _sigmoid(x):
    return 1.0 / (1.0 + jnp.exp(-x))


def _softplus(x):
    return jnp.maximum(x, 0.0) + jnp.log1p(jnp.exp(-jnp.abs(x)))


def _normmod(x, g, sc, sh):
    ms = jnp.mean(x * x, axis=-1, keepdims=True)
    return (x * lax.rsqrt(ms + EPS) * g) * (1.0 + sc) + sh


def _rms(x, g):
    ms = jnp.mean(x * x, axis=-1, keepdims=True)
    return x * lax.rsqrt(ms + EPS) * g


def _dot(a, b):
    return jnp.dot(a, b, preferred_element_type=F32)


def _dot_t(a, b):
    return lax.dot_general(a, b, (((1,), (1,)), ((), ())), preferred_element_type=F32)


def _const_spec(shape):
    n = len(shape)
    return pl.BlockSpec(shape, lambda *_: (0,) * n)


def _row_spec(tm, width, nt):
    return pl.BlockSpec((tm, width), lambda b, t: (b * nt + t, 0))


def _batch_vec_spec(width):
    return pl.BlockSpec((None, 1, width), lambda b, t: (b, 0, 0))


def _mod_kernel(c_ref, w_ref, b_ref, o_ref):
    cs = _silu(c_ref[...]).astype(BF16)
    o_ref[...] = _dot(cs, w_ref[...].astype(BF16)) + b_ref[...]


def _modulation(c_all, ada_w, ada_b):
    depth, d, n = ada_w.shape
    nb = c_all.shape[0]
    tn = 1536
    return pl.pallas_call(
        _mod_kernel,
        grid=(depth, n // tn),
        in_specs=[pl.BlockSpec((nb, d), lambda l, j: (0, 0)),
                  pl.BlockSpec((None, d, tn), lambda l, j: (l, 0, j)),
                  pl.BlockSpec((None, 1, tn), lambda l, j: (l, 0, j))],
        out_specs=pl.BlockSpec((None, nb, tn), lambda l, j: (l, 0, j)),
        out_shape=jax.ShapeDtypeStruct((depth, nb, n), F32),
        compiler_params=_cp(2),
        name="adaln_mod",
    )(c_all, ada_w, ada_b.reshape(depth, 1, n))


def _gdn_kernel(x_ref, sc_ref, sh_ref, gt_ref, ng_ref, hist_ref, s0_ref, wqkv_ref, wz_ref, wab_ref,
                cw_ref, alog_ref, dtb_ref, on_ref, wo_ref,
                xo_ref, nh_ref, so_ref,
                xp_scr, cv_scr, s_scr, o_scr, *, tm, rows):
    t = pl.program_id(1)
    nt = pl.num_programs(1)
    C = GDN_CHUNK
    K = GDN_CONV

    @pl.when(t == 0)
    def _():
        s_scr[...] = s0_ref[...]
        xp_scr[CONV_PAD - (K - 1):CONV_PAD, :] = hist_ref[...]

    x = x_ref[...]
    h = _normmod(x, ng_ref[...], sc_ref[...], sh_ref[...]).astype(BF16)

    qkv = _dot(h, wqkv_ref[...])
    xp_scr[CONV_PAD:CONV_PAD + tm, :] = qkv
    cw = cw_ref[...]
    conv = qkv * cw[K - 1:K]
    for j in range(K - 1):
        off = CONV_PAD - (K - 1) + j
        conv = conv + xp_scr[off:off + tm, :] * cw[j:j + 1]
    tail = qkv[tm - (K - 1):tm, :]
    xp_scr[CONV_PAD - (K - 1):CONV_PAD, :] = tail

    @pl.when(t == nt - 1)
    def _():
        nh_ref[...] = tail

    cv_scr[0:tm, :] = _silu(conv)
    if rows > tm:
        cv_scr[tm:rows, :] = jnp.zeros((rows - tm, GDN_QKV), F32)

    ab = _dot(h, wab_ref[...])
    g = -jnp.exp(alog_ref[...]) * _softplus(ab[:, :LANES] + dtb_ref[...])
    beta = _sigmoid(ab[:, LANES:])
    if rows > tm:
        zpad = jnp.zeros((rows - tm, LANES), F32)
        g = jnp.concatenate([g, zpad], axis=0)
        beta = jnp.concatenate([beta, zpad], axis=0)

    ri = lax.broadcasted_iota(jnp.int32, (rows, rows), 0)
    ci = lax.broadcasted_iota(jnp.int32, (rows, rows), 1)
    tril = jnp.where((ri >= ci) & ((ri // C) == (ci // C)), 1.0, 0.0).astype(F32)
    gc = jnp.dot(tril, g, precision=HIGHEST, preferred_element_type=F32)
    gct = gc.T

    ii = lax.broadcasted_iota(jnp.int32, (C, C), 0)
    jj = lax.broadcasted_iota(jnp.int32, (C, C), 1)
    same = []
    s = 8
    while s <= C:
        same.append((ii // s) == (jj // s))
        s *= 2
    scale = GDN_DK ** -0.5
    HK = GDN_HEADS * GDN_DK

    heads = range(GDN_HEADS)
    on = on_ref[...]
    S = [s_scr[hd] for hd in heads]
    for c in range(rows // C):
        rs = slice(c * C, (c + 1) * C)
        q, k, v, gcol, bcol, eg, kb, A, Ab, qk, N, P = ([None] * GDN_HEADS for _ in range(12))
        for hd in heads:
            qh = cv_scr[rs, hd * GDN_DK:(hd + 1) * GDN_DK]
            kh = cv_scr[rs, HK + hd * GDN_DK:HK + (hd + 1) * GDN_DK]
            v[hd] = cv_scr[rs, 2 * HK + hd * GDN_DV:2 * HK + (hd + 1) * GDN_DV]
            q[hd] = qh * lax.rsqrt(jnp.sum(qh * qh, axis=-1, keepdims=True) + EPS) * scale
            k[hd] = kh * lax.rsqrt(jnp.sum(kh * kh, axis=-1, keepdims=True) + EPS)
            gcol[hd] = gc[rs, hd:hd + 1]
            grow = gct[hd:hd + 1, rs]
            bcol[hd] = beta[rs, hd:hd + 1]
            decay = jnp.exp(jnp.where(ii >= jj, gcol[hd] - grow, NEG))
            kb[hd] = k[hd] * bcol[hd]
            kq = _dot_t(jnp.concatenate([kb[hd], q[hd]], axis=0).astype(BF16), k[hd].astype(BF16))
            A[hd] = jnp.where(ii > jj, kq[:C] * decay, 0.0)
            qk[hd] = (kq[C:] * decay).astype(BF16)
            Ab[hd] = A[hd].astype(BF16)
        for hd in heads:
            D = jnp.where(same[0], Ab[hd], 0)
            N[hd] = -jnp.where(same[0], A[hd], 0.0)
            P[hd] = _dot(D, D)
        for hd in heads:
            Pb = P[hd].astype(BF16)
            N[hd] = N[hd] + P[hd] + _dot(N[hd].astype(BF16), Pb)
            P[hd] = _dot(Pb, Pb)
        for hd in heads:
            N[hd] = N[hd] + P[hd] + _dot(N[hd].astype(BF16), P[hd].astype(BF16))
        for lvl in range(1, len(same)):
            sel = same[lvl] & jnp.logical_not(same[lvl - 1])
            X = [None] * GDN_HEADS
            for hd in heads:
                off = jnp.where(sel, Ab[hd], 0)
                X[hd] = off.astype(F32) + _dot(off, N[hd].astype(BF16))
            for hd in heads:
                N[hd] = N[hd] - X[hd] - _dot(N[hd].astype(BF16), X[hd].astype(BF16))
        w_v, w_k = [None] * GDN_HEADS, [None] * GDN_HEADS
        for hd in heads:
            eg[hd] = jnp.exp(gcol[hd])
            rhs = jnp.concatenate([v[hd] * bcol[hd], kb[hd] * eg[hd]], axis=1)
            sol = rhs + _dot(N[hd].astype(BF16), rhs.astype(BF16))
            w_v[hd], w_k[hd] = sol[:, :GDN_DV], sol[:, GDN_DV:].astype(BF16)
        Sb = [S[hd].astype(BF16) for hd in heads]
        ub = [(w_v[hd] - _dot(w_k[hd], Sb[hd])).astype(BF16) for hd in heads]
        for hd in heads:
            o = _dot((q[hd] * eg[hd]).astype(BF16), Sb[hd]) + _dot(qk[hd], ub[hd])
            gl = gcol[hd][C - 1:C, :]
            kdec = (k[hd] * jnp.exp(gl - gcol[hd])).T.astype(BF16)
            S[hd] = S[hd] * jnp.exp(gl) + _dot(kdec, ub[hd])
            o_scr[rs, hd * GDN_DV:(hd + 1) * GDN_DV] = _rms(o, on)
    for hd in heads:
        s_scr[hd] = S[hd]

    z = _dot(h, wz_ref[...])
    og = (o_scr[0:tm, :] * _silu(z)).astype(BF16)
    xo_ref[...] = x + gt_ref[...] * _dot(og, wo_ref[...])

    @pl.when(t == nt - 1)
    def _():
        so_ref[...] = s_scr[...]


def _gdn_layer(x2d, nb, T, sc, sh, gt, ng, hist, s0, w_in, conv_w, a_log, dt_bias, o_norm, w_o, tm):
    nt = T // tm
    rows = max(tm, GDN_CHUNK)
    H, dv = GDN_HEADS, GDN_DV
    wqkv = w_in[:, :GDN_QKV].astype(BF16)
    wz = w_in[:, GDN_QKV:GDN_QKV + H * dv].astype(BF16)
    wa = w_in[:, GDN_QKV + H * dv:GDN_QKV + H * dv + H]
    wb = w_in[:, GDN_QKV + H * dv + H:]
    lane_pad = ((0, 0), (0, LANES - H))
    wab = jnp.concatenate([jnp.pad(wa, lane_pad), jnp.pad(wb, lane_pad)], axis=1).astype(BF16)
    alog = jnp.pad(a_log.reshape(1, H), lane_pad)
    dtb = jnp.pad(dt_bias.reshape(1, H), lane_pad)
    kern = functools.partial(_gdn_kernel, tm=tm, rows=rows)
    return pl.pallas_call(
        kern,
        grid=(nb, nt),
        in_specs=[_row_spec(tm, D_MODEL, nt), _batch_vec_spec(D_MODEL), _batch_vec_spec(D_MODEL),
                  _batch_vec_spec(D_MODEL), _const_spec((1, D_MODEL)),
                  pl.BlockSpec((None, GDN_CONV - 1, GDN_QKV), lambda b, t: (b, 0, 0)),
                  pl.BlockSpec((None, H, GDN_DK, dv), lambda b, t: (b, 0, 0, 0)),
                  _const_spec((D_MODEL, GDN_QKV)), _const_spec((D_MODEL, H * dv)),
                  _const_spec((D_MODEL, 2 * LANES)), _const_spec((GDN_CONV, GDN_QKV)),
                  _const_spec((1, LANES)), _const_spec((1, LANES)), _const_spec((1, dv)),
                  _const_spec((H * dv, D_MODEL))],
        out_specs=[_row_spec(tm, D_MODEL, nt),
                   pl.BlockSpec((None, GDN_CONV - 1, GDN_QKV), lambda b, t: (b, 0, 0)),
                   pl.BlockSpec((None, H, GDN_DK, dv), lambda b, t: (b, 0, 0, 0))],
        out_shape=[jax.ShapeDtypeStruct((nb * T, D_MODEL), F32),
                   jax.ShapeDtypeStruct((nb, GDN_CONV - 1, GDN_QKV), F32),
                   jax.ShapeDtypeStruct((nb, H, GDN_DK, dv), F32)],
        scratch_shapes=[pltpu.VMEM((CONV_PAD + tm, GDN_QKV), F32),
                        pltpu.VMEM((rows, GDN_QKV), F32),
                        pltpu.VMEM((H, GDN_DK, dv), F32),
                        pltpu.VMEM((rows, H * dv), F32)],
        compiler_params=_cp(2),
        name="gdn_layer",
    )(x2d, sc, sh, gt, ng.reshape(1, D_MODEL), hist, s0, wqkv, wz, wab, conv_w, alog, dtb,
      o_norm.reshape(1, dv), w_o.astype(BF16))


def _ffn_kernel(x_ref, sc_ref, sh_ref, gt_ref, ng_ref, hist_ref, wg_ref, wu_ref, cw_ref, cb_ref, wo_ref, fn_ref,
                xo_ref, nh_ref, xp_scr, *, tm, final):
    t = pl.program_id(1)
    nt = pl.num_programs(1)
    K = FFN_CONV

    @pl.when(t == 0)
    def _():
        xp_scr[CONV_PAD - (K - 1):CONV_PAD, :] = hist_ref[...]

    x = x_ref[...]
    h = _normmod(x, ng_ref[...], sc_ref[...], sh_ref[...]).astype(BF16)
    gate = _dot(h, wg_ref[...])
    xp_scr[CONV_PAD:CONV_PAD + tm, :] = gate
    cw = cw_ref[...]
    conv = gate * cw[K - 1:K]
    for j in range(K - 1):
        off = CONV_PAD - (K - 1) + j
        conv = conv + xp_scr[off:off + tm, :] * cw[j:j + 1]
    tail = gate[tm - (K - 1):tm, :]
    xp_scr[CONV_PAD - (K - 1):CONV_PAD, :] = tail

    @pl.when(t == nt - 1)
    def _():
        nh_ref[...] = tail

    up = _dot(h, wu_ref[...])
    act = (_silu(conv + cb_ref[...]) * up).astype(BF16)
    y = x + gt_ref[...] * _dot(act, wo_ref[...])
    if final:
        y = _rms(y, fn_ref[...])
    xo_ref[...] = y


def _ffn_layer(x2d, nb, T, sc, sh, gt, ng, hist, w_in, conv_w, conv_b, w_out, final_g, final, tm):
    nt = T // tm
    wg = w_in[:, :D_FF].astype(BF16)
    wu = w_in[:, D_FF:].astype(BF16)
    kern = functools.partial(_ffn_kernel, tm=tm, final=final)
    return pl.pallas_call(
        kern,
        grid=(nb, nt),
        in_specs=[_row_spec(tm, D_MODEL, nt), _batch_vec_spec(D_MODEL), _batch_vec_spec(D_MODEL),
                  _batch_vec_spec(D_MODEL), _const_spec((1, D_MODEL)),
                  pl.BlockSpec((None, FFN_CONV - 1, D_FF), lambda b, t: (b, 0, 0)),
                  _const_spec((D_MODEL, D_FF)), _const_spec((D_MODEL, D_FF)),
                  _const_spec((FFN_CONV, D_FF)), _const_spec((1, D_FF)),
                  _const_spec((D_FF, D_MODEL)), _const_spec((1, D_MODEL))],
        out_specs=[_row_spec(tm, D_MODEL, nt),
                   pl.BlockSpec((None, FFN_CONV - 1, D_FF), lambda b, t: (b, 0, 0))],
        out_shape=[jax.ShapeDtypeStruct((nb * T, D_MODEL), F32),
                   jax.ShapeDtypeStruct((nb, FFN_CONV - 1, D_FF), F32)],
        scratch_shapes=[pltpu.VMEM((CONV_PAD + tm, D_FF), F32)],
        compiler_params=_cp(2),
        name="conv_ffn_layer",
    )(x2d, sc, sh, gt, ng.reshape(1, D_MODEL), hist, wg, wu, conv_w, conv_b.reshape(1, D_FF),
      w_out.astype(BF16), final_g.reshape(1, D_MODEL))


def _proj_res_kernel(a_ref, w_ref, x_ref, gt_ref, o_ref):
    o_ref[...] = x_ref[...] + gt_ref[...] * _dot(a_ref[...], w_ref[...])


def _proj_res(a2d, w, x2d, gt, nb, T, tm):
    nt = T // tm
    kdim = a2d.shape[1]
    return pl.pallas_call(
        _proj_res_kernel,
        grid=(nb, nt),
        in_specs=[_row_spec(tm, kdim, nt), _const_spec((kdim, D_MODEL)), _row_spec(tm, D_MODEL, nt),
                  _batch_vec_spec(D_MODEL)],
        out_specs=_row_spec(tm, D_MODEL, nt),
        out_shape=jax.ShapeDtypeStruct((nb * T, D_MODEL), F32),
        compiler_params=_cp(2),
        name="attn_out_proj",
    )(a2d, w.astype(BF16), x2d, gt)


def _mla_q_kernel(x_ref, sc_ref, sh_ref, ng_ref, wq_ref, wkv_ref, wr_ref, qn_ref, kvn_ref, wu1_ref, wu2_ref,
                  cos_ref, sin_ref, q_ref, ckv_ref, kr_ref):
    h = _normmod(x_ref[...], ng_ref[...], sc_ref[...], sh_ref[...]).astype(BF16)
    cq = _rms(_dot(h, wq_ref[...]), qn_ref[...]).astype(BF16)
    ckv_ref[...] = _rms(_dot(h, wkv_ref[...]), kvn_ref[...])
    cs = cos_ref[...]
    sn = sin_ref[...]
    r = _dot(h, wr_ref[...])
    kr_ref[...] = r[:, :MLA_SLAB] * cs + r[:, MLA_SLAB:] * sn
    q1 = _dot(cq, wu1_ref[...])
    q2 = _dot(cq, wu2_ref[...])
    for hd in range(MLA_HEADS):
        sl = slice(hd * MLA_SLAB, (hd + 1) * MLA_SLAB)
        q_ref[:, sl] = (q1[:, sl] * cs + q2[:, sl] * sn).astype(BF16)


def _rope_tables(pos):
    half = MLA_ROPE // 2
    inv = ROPE_THETA ** (-jnp.arange(half, dtype=F32) / half)
    ang = pos.astype(F32)[:, None] * inv[None, :]
    cos, sin = jnp.cos(ang), jnp.sin(ang)
    n = pos.shape[0]
    pad = jnp.zeros((n, MLA_SLAB - MLA_NOPE - MLA_ROPE), F32)
    cs = jnp.concatenate([jnp.ones((n, MLA_NOPE), F32), cos, cos, pad], axis=1)
    sn = jnp.concatenate([jnp.zeros((n, MLA_NOPE), F32), -sin, sin, pad], axis=1)
    return cs, sn


def _slab_weights(w_q_up, w_kv_up, w_r):
    half = MLA_ROPE // 2
    hq = MLA_NOPE + MLA_ROPE
    padw = MLA_SLAB - hq
    wq = w_q_up.reshape(MLA_Q_LORA, MLA_HEADS, hq)
    wq_n, wq_r = wq[..., :MLA_NOPE], wq[..., MLA_NOPE:]
    wq_sw = jnp.concatenate([wq_r[..., half:], wq_r[..., :half]], axis=-1)
    zq = jnp.zeros((MLA_Q_LORA, MLA_HEADS, padw), F32)
    wu1 = jnp.concatenate([wq_n, wq_r, zq], axis=-1).reshape(MLA_Q_LORA, MLA_HEADS * MLA_SLAB)
    wu2 = jnp.concatenate([jnp.zeros_like(wq_n), wq_sw, zq], axis=-1).reshape(MLA_Q_LORA, MLA_HEADS * MLA_SLAB)
    wkv = w_kv_up.reshape(MLA_KV_LORA, MLA_HEADS, MLA_NOPE + MLA_V)
    wk = jnp.concatenate([wkv[..., :MLA_NOPE], jnp.zeros((MLA_KV_LORA, MLA_HEADS, MLA_SLAB - MLA_NOPE), F32)],
                         axis=-1).reshape(MLA_KV_LORA, MLA_HEADS * MLA_SLAB)
    wv = wkv[..., MLA_NOPE:].reshape(MLA_KV_LORA, MLA_HEADS * MLA_V)
    d = w_r.shape[0]
    wr_sw = jnp.concatenate([w_r[:, half:], w_r[:, :half]], axis=1)
    zl = jnp.zeros((d, MLA_NOPE), F32)
    zr = jnp.zeros((d, padw), F32)
    wr2 = jnp.concatenate([zl, w_r, zr, zl, wr_sw, zr], axis=1)
    return wu1.astype(BF16), wu2.astype(BF16), wk.astype(BF16), wv.astype(BF16), wr2.astype(BF16)


def _mla_q(x2d, nb, T, sc, sh, ng, wq, wkv, wr2, qn, kvn, wu1, wu2, cs, sn, tm):
    nt = T // tm
    qw = MLA_HEADS * MLA_SLAB
    return pl.pallas_call(
        _mla_q_kernel,
        grid=(nb, nt),
        in_specs=[_row_spec(tm, D_MODEL, nt), _batch_vec_spec(D_MODEL), _batch_vec_spec(D_MODEL),
                  _const_spec((1, D_MODEL)), _const_spec((D_MODEL, MLA_Q_LORA)),
                  _const_spec((D_MODEL, MLA_KV_LORA)), _const_spec((D_MODEL, 2 * MLA_SLAB)),
                  _const_spec((1, MLA_Q_LORA)), _const_spec((1, MLA_KV_LORA)),
                  _const_spec((MLA_Q_LORA, qw)), _const_spec((MLA_Q_LORA, qw)),
                  pl.BlockSpec((tm, MLA_SLAB), lambda b, t: (t, 0)),
                  pl.BlockSpec((tm, MLA_SLAB), lambda b, t: (t, 0))],
        out_specs=[_row_spec(tm, qw, nt), _row_spec(tm, MLA_KV_LORA, nt), _row_spec(tm, MLA_SLAB, nt)],
        out_shape=[jax.ShapeDtypeStruct((nb * T, qw), BF16),
                   jax.ShapeDtypeStruct((nb * T, MLA_KV_LORA), F32),
                   jax.ShapeDtypeStruct((nb * T, MLA_SLAB), F32)],
        compiler_params=_cp(2),
        name="mla_q_proj",
    )(x2d, sc, sh, ng.reshape(1, D_MODEL), wq, wkv, wr2, qn.reshape(1, -1), kvn.reshape(1, -1), wu1, wu2, cs, sn)


def _mla_kv_kernel(ckv_ref, kr_ref, wk_ref, wv_ref, k_ref, v_ref):
    c = ckv_ref[...].astype(BF16)
    kn = _dot(c, wk_ref[...])
    kr = kr_ref[...]
    for hd in range(MLA_HEADS):
        sl = slice(hd * MLA_SLAB, (hd + 1) * MLA_SLAB)
        k_ref[:, sl] = (kn[:, sl] + kr).astype(BF16)
    v_ref[...] = _dot(c, wv_ref[...]).astype(BF16)


def _mla_kv(ckv2d, kr2d, wk, wv, tm):
    rows = ckv2d.shape[0]
    kw = MLA_HEADS * MLA_SLAB
    vw = MLA_HEADS * MLA_V
    return pl.pallas_call(
        _mla_kv_kernel,
        grid=(rows // tm,),
        in_specs=[pl.BlockSpec((tm, MLA_KV_LORA), lambda i: (i, 0)), pl.BlockSpec((tm, MLA_SLAB), lambda i: (i, 0)),
                  _const_spec((MLA_KV_LORA, kw)), _const_spec((MLA_KV_LORA, vw))],
        out_specs=[pl.BlockSpec((tm, kw), lambda i: (i, 0)), pl.BlockSpec((tm, vw), lambda i: (i, 0))],
        out_shape=[jax.ShapeDtypeStruct((rows, kw), BF16), jax.ShapeDtypeStruct((rows, vw), BF16)],
        compiler_params=_cp(1),
        name="mla_kv_up",
    )(ckv2d, kr2d, wk, wv)


def _mla_attn_kernel(q_ref, k_ref, v_ref, o_ref, *, tq, tk, causal, n_kv_full):
    i = pl.program_id(2)
    scale = (MLA_NOPE + MLA_ROPE) ** -0.5
    q = q_ref[...]
    qs = [q[:, :MLA_SLAB], q[:, MLA_SLAB:]]
    n_kv = (i + 1) if causal else n_kv_full

    def body(j, carry):
        off = pl.multiple_of(j * tk, tk)
        kblk = k_ref[pl.ds(off, tk), :]
        vblk = v_ref[pl.ds(off, tk), :]
        if causal:
            row = lax.broadcasted_iota(jnp.int32, (tq, tk), 0) + i * tq
            col = lax.broadcasted_iota(jnp.int32, (tq, tk), 1) + off
            keep = (col // CHUNK) <= (row // CHUNK)
        out = []
        for hh in range(2):
            m, l, acc = carry[hh]
            s = _dot_t(qs[hh], kblk[:, hh * MLA_SLAB:(hh + 1) * MLA_SLAB]) * scale
            if causal:
                s = jnp.where(keep, s, NEG)
            m_new = jnp.maximum(m, jnp.max(s, axis=-1, keepdims=True))
            p = jnp.exp(s - m_new)
            alpha = jnp.exp(m - m_new)
            l = alpha * l + jnp.sum(p, axis=-1, keepdims=True)
            acc = alpha * acc + _dot(p.astype(BF16), vblk)
            out.append((m_new, l, acc))
        return tuple(out)

    init = tuple((jnp.full((tq, 1), NEG, F32), jnp.zeros((tq, 1), F32), jnp.zeros((tq, 2 * MLA_V), F32))
                 for _ in range(2))
    res = lax.fori_loop(0, n_kv, body, init)
    o0 = res[0][2] / res[0][1]
    o1 = res[1][2] / res[1][1]
    lane = lax.broadcasted_iota(jnp.int32, (tq, 2 * MLA_V), 1)
    o_ref[...] = jnp.where(lane < MLA_V, o0, o1).astype(BF16)


def _mla_attn(q3, k3, v3, tq, tk, causal):
    nb, Tq, _ = q3.shape
    Tk = k3.shape[1]
    npairs = MLA_HEADS // 2
    kern = functools.partial(_mla_attn_kernel, tq=tq, tk=tk, causal=causal, n_kv_full=Tk // tk)
    return pl.pallas_call(
        kern,
        grid=(nb, npairs, Tq // tq),
        in_specs=[pl.BlockSpec((None, tq, 2 * MLA_SLAB), lambda b, p, i: (b, i, p)),
                  pl.BlockSpec((None, Tk, 2 * MLA_SLAB), lambda b, p, i: (b, 0, p)),
                  pl.BlockSpec((None, Tk, 2 * MLA_V), lambda b, p, i: (b, 0, p))],
        out_specs=pl.BlockSpec((None, tq, 2 * MLA_V), lambda b, p, i: (b, i, p)),
        out_shape=jax.ShapeDtypeStruct((nb, Tq, MLA_HEADS * MLA_V), BF16),
        compiler_params=_cp(3),
        name="mla_attention",
    )(q3, k3, v3)


def _swa_proj_kernel(x_ref, sc_ref, sh_ref, ng_ref, w_ref, q_ref, kv_ref, last_ref, *, tm, keep):
    h = _normmod(x_ref[...], ng_ref[...], sc_ref[...], sh_ref[...]).astype(BF16)
    proj = _dot(h, w_ref[...])
    nq = SWA_HEADS * SWA_HD
    q_ref[...] = proj[:, :nq].astype(BF16)
    kv = proj[:, nq:]
    kv_ref[...] = kv.astype(BF16)
    last_ref[...] = kv[tm - keep:, :]


def _swa_proj(x2d, nb, T, sc, sh, ng, w_in, tm):
    nt = T // tm
    keep = min(WINDOW, T)
    nq = SWA_HEADS * SWA_HD
    nkv = 2 * SWA_KV_HEADS * SWA_HD
    kern = functools.partial(_swa_proj_kernel, tm=tm, keep=keep)
    return pl.pallas_call(
        kern,
        grid=(nb, nt),
        in_specs=[_row_spec(tm, D_MODEL, nt), _batch_vec_spec(D_MODEL), _batch_vec_spec(D_MODEL),
                  _const_spec((1, D_MODEL)), _const_spec((D_MODEL, nq + nkv))],
        out_specs=[_row_spec(tm, nq, nt), _row_spec(tm, nkv, nt),
                   pl.BlockSpec((None, keep, nkv), lambda b, t: (b, 0, 0))],
        out_shape=[jax.ShapeDtypeStruct((nb * T, nq), BF16), jax.ShapeDtypeStruct((nb * T, nkv), BF16),
                   jax.ShapeDtypeStruct((nb, keep, nkv), F32)],
        compiler_params=_cp(2),
        name="swa_proj",
    )(x2d, sc, sh, ng.reshape(1, D_MODEL), w_in.astype(BF16))


def _t5_bias_kernel(bucket_ref, table_ref, o_ref):
    hd = pl.program_id(0)
    bucket = bucket_ref[...]
    acc = jnp.full(bucket.shape, NEG, F32)
    for b in range(REL_BUCKETS):
        acc = jnp.where(bucket == b, table_ref[b, hd], acc)
    o_ref[...] = acc


def _t5_bias(rel_bias, q_pos, k_pos, valid):
    n = q_pos[:, None] - k_pos[None, :]
    half = REL_BUCKETS // 2
    exact = half // 2
    side = jnp.where(n < 0, half, 0)
    n = jnp.abs(n)
    log_b = exact + (jnp.log(jnp.maximum(n, 1).astype(F32) / exact)
                     / math.log(REL_MAX_DIST / exact) * (half - exact)).astype(jnp.int32)
    bucket = side + jnp.where(n < exact, n, jnp.minimum(log_b, half - 1))
    bucket = jnp.where(valid, bucket, -1).astype(jnp.int32)
    tq, tk = bucket.shape
    return pl.pallas_call(
        _t5_bias_kernel,
        grid=(SWA_HEADS,),
        in_specs=[pl.BlockSpec((tq, tk), lambda h: (0, 0)),
                  pl.BlockSpec(memory_space=pltpu.SMEM)],
        out_specs=pl.BlockSpec((None, tq, tk), lambda h: (h, 0, 0)),
        out_shape=jax.ShapeDtypeStruct((SWA_HEADS, tq, tk), F32),
        compiler_params=_cp(1),
        name="t5_bias",
    )(bucket, rel_bias)


def _swa_attn_kernel(q_ref, kv_ref, bias_ref, sink_ref, o_ref, *, tq, lk, npad):
    i = pl.program_id(1)
    off = pl.multiple_of(i * tq, tq)
    kvw = kv_ref[pl.ds(off, lk), :]
    nk = SWA_KV_HEADS * SWA_HD
    q = q_ref[...]
    if npad:
        col = lax.broadcasted_iota(jnp.int32, (tq, lk), 1) + off
        real = col >= npad
    outs = []
    for hd in range(SWA_HEADS):
        kvh = hd // SWA_GROUP
        qh = q[:, hd * SWA_HD:(hd + 1) * SWA_HD]
        kh = kvw[:, kvh * SWA_HD:(kvh + 1) * SWA_HD]
        vh = kvw[:, nk + kvh * SWA_HD:nk + (kvh + 1) * SWA_HD]
        s = _dot_t(qh, kh) * (SWA_HD ** -0.5) + bias_ref[hd]
        if npad:
            s = jnp.where(real, s, NEG)
        sink = sink_ref[hd]
        m = jnp.maximum(jnp.max(s, axis=-1, keepdims=True), sink)
        p = jnp.exp(s - m)
        p = p / (jnp.sum(p, axis=-1, keepdims=True) + jnp.exp(sink - m))
        outs.append(_dot(p.astype(BF16), vh))
    o_ref[...] = jnp.concatenate(outs, axis=-1).astype(BF16)


def _swa_attn(q3, kv3, bias, sinks, tq, lk, npad):
    nb, Tq, nq = q3.shape
    Tk, nkv = kv3.shape[1:]
    kern = functools.partial(_swa_attn_kernel, tq=tq, lk=lk, npad=npad)
    return pl.pallas_call(
        kern,
        grid=(nb, Tq // tq),
        in_specs=[pl.BlockSpec((None, tq, nq), lambda b, i: (b, i, 0)),
                  pl.BlockSpec((None, Tk, nkv), lambda b, i: (b, 0, 0)),
                  _const_spec((SWA_HEADS, tq, lk)),
                  pl.BlockSpec(memory_space=pltpu.SMEM)],
        out_specs=pl.BlockSpec((None, tq, nq), lambda b, i: (b, i, 0)),
        out_shape=jax.ShapeDtypeStruct((nb, Tq, nq), BF16),
        compiler_params=_cp(2),
        name="swa_attention",
    )(q3, kv3, bias, sinks)


def _trunk(x, mods, prompt, past_len, st, p, tm, ffn_tm):
    B, T, _ = x.shape
    x2d = x.reshape(B * T, D_MODEL)
    pos = jnp.arange(T, dtype=jnp.int32) + (0 if prompt else past_len)
    new = dict(gdn_conv=[], gdn_S=[], mla_latent=[], mla_krope=[], swa_k=[], swa_v=[], ffn_conv=[])
    depth = p['ada_w'].shape[0]
    for layer in range(depth):
        kind, slot = layer % 3, layer // 3
        sh1, sc1, g1, sh2, sc2, g2 = [mods[layer, :, j * D_MODEL:(j + 1) * D_MODEL][:, None, :] for j in range(6)]
        ng = p['norm1'][layer]
        if kind == 0:
            x2d, conv_h, S = _gdn_layer(x2d, B, T, sc1, sh1, g1, ng, st['gdn_conv'][slot], st['gdn_S'][slot],
                                        p['gdn_w_in'][slot], p['gdn_conv_w'][slot], p['gdn_a_log'][slot],
                                        p['gdn_dt_bias'][slot], p['gdn_o_norm'][slot], p['gdn_w_o'][slot], tm)
            new['gdn_conv'].append(conv_h)
            new['gdn_S'].append(S)
        elif kind == 1:
            w_in = p['mla_w_in'][slot]
            wq = w_in[:, :MLA_Q_LORA].astype(BF16)
            wkv = w_in[:, MLA_Q_LORA:MLA_Q_LORA + MLA_KV_LORA].astype(BF16)
            wu1, wu2, wk, wv, wr2 = _slab_weights(p['mla_w_q_up'][slot], p['mla_w_kv_up'][slot],
                                                  w_in[:, MLA_Q_LORA + MLA_KV_LORA:])
            cs, sn = _rope_tables(pos)
            qcat, ckv, kr = _mla_q(x2d, B, T, sc1, sh1, ng, wq, wkv, wr2, p['mla_q_norm'][slot],
                                   p['mla_kv_norm'][slot], wu1, wu2, cs, sn, tm)
            ckv3 = ckv.reshape(B, T, MLA_KV_LORA)
            kr3 = kr.reshape(B, T, MLA_SLAB)
            if prompt:
                ckv_all, kr_all, Tk = ckv, kr, T
                kv_tm, tq, tk = tm, tm, tm
            else:
                assert past_len % CHUNK + T <= CHUNK
                cache_kr = jnp.pad(st['mla_krope'][slot].astype(F32),
                                   ((0, 0), (0, 0), (MLA_NOPE, MLA_SLAB - MLA_NOPE - MLA_ROPE)))
                Tk = past_len + T
                ckv_all = jnp.concatenate([st['mla_latent'][slot].astype(F32), ckv3], axis=1).reshape(B * Tk, -1)
                kr_all = jnp.concatenate([cache_kr, kr3], axis=1).reshape(B * Tk, MLA_SLAB)
                kv_tm, tq, tk = Tk, T, Tk
            kcat, vv = _mla_kv(ckv_all, kr_all, wk, wv, kv_tm)
            o = _mla_attn(qcat.reshape(B, T, -1), kcat.reshape(B, Tk, -1), vv.reshape(B, Tk, -1), tq, tk, prompt)
            x2d = _proj_res(o.reshape(B * T, -1), p['mla_w_o'][slot], x2d, g1, B, T, tm)
            new['mla_latent'].append(ckv3)
            new['mla_krope'].append(kr3[:, :, MLA_NOPE:MLA_NOPE + MLA_ROPE])
        else:
            q, kvb, kv_last = _swa_proj(x2d, B, T, sc1, sh1, ng, p['swa_w_in'][slot], tm)
            nk = SWA_KV_HEADS * SWA_HD
            if prompt:
                tq = 2 * CHUNK
                lk = tq + WINDOW
                rq = WINDOW + jnp.arange(tq)
                rk = jnp.arange(lk)
                d = rq[:, None] // CHUNK - rk[None, :] // CHUNK
                bias = _t5_bias(p['rel_bias'], rq, rk, (d >= 0) & (d <= WIN_CHUNKS))
                kv3 = jnp.pad(kvb.reshape(B, T, 2 * nk), ((0, 0), (WINDOW, 0), (0, 0)))
                heads = _swa_attn(q.reshape(B, T, -1), kv3, bias, p['swa_sinks'][slot], tq, lk, WINDOW)
                new_k, new_v = kv_last[:, :, :nk], kv_last[:, :, nk:]
            else:
                win = st['swa_k'].shape[2]
                ck = st['swa_k'][slot].reshape(B, win, nk).astype(F32)
                cv = st['swa_v'][slot].reshape(B, win, nk).astype(F32)
                k_all = jnp.concatenate([ck, kv_last[:, :, :nk]], axis=1)
                v_all = jnp.concatenate([cv, kv_last[:, :, nk:]], axis=1)
                k_pos = jnp.concatenate([past_len - win + jnp.arange(win, dtype=jnp.int32), pos])
                d = (pos // CHUNK)[:, None] - (k_pos // CHUNK)[None, :]
                bias = _t5_bias(p['rel_bias'], pos, k_pos, (d >= 0) & (d <= WIN_CHUNKS))
                kv3 = jnp.concatenate([k_all, v_all], axis=-1).astype(BF16)
                heads = _swa_attn(q.reshape(B, T, -1), kv3, bias, p['swa_sinks'][slot], T, win + T, 0)
                new_k, new_v = k_all[:, -win:], v_all[:, -win:]
            x2d = _proj_res(heads.reshape(B * T, -1), p['swa_w_o'][slot], x2d, g1, B, T, tm)
            new['swa_k'].append(new_k.reshape(B, -1, SWA_KV_HEADS, SWA_HD))
            new['swa_v'].append(new_v.reshape(B, -1, SWA_KV_HEADS, SWA_HD))
        x2d, f_hist = _ffn_layer(x2d, B, T, sc2, sh2, g2, p['norm2'][layer], st['ffn_conv'][layer],
                                 p['ffn_w_in'][layer], p['ffn_conv_w'][layer], p['ffn_conv_b'][layer],
                                 p['ffn_w_out'][layer], p['final_norm'], layer == depth - 1, ffn_tm)
        new['ffn_conv'].append(f_hist)
    return x2d.reshape(B, T, D_MODEL), {name: jnp.stack(rows) for name, rows in new.items()}


def kernel(x_prompt, x_sample, c_prompt, c_sample, state_gdn_conv, state_gdn_S, cache_mla_latent, cache_mla_krope, cache_swa_k, cache_swa_v, state_ffn_conv, ada_w, ada_b, norm1, norm2, final_norm, gdn_w_in, gdn_conv_w, gdn_a_log, gdn_dt_bias, gdn_o_norm, gdn_w_o, mla_w_in, mla_q_norm, mla_kv_norm, mla_w_q_up, mla_w_kv_up, mla_w_o, swa_w_in, swa_sinks, swa_w_o, rel_bias, ffn_w_in, ffn_conv_w, ffn_conv_b, ffn_w_out):
    p = dict(ada_w=ada_w, ada_b=ada_b, norm1=norm1, norm2=norm2, final_norm=final_norm,
             gdn_w_in=gdn_w_in, gdn_conv_w=gdn_conv_w, gdn_a_log=gdn_a_log, gdn_dt_bias=gdn_dt_bias,
             gdn_o_norm=gdn_o_norm, gdn_w_o=gdn_w_o, mla_w_in=mla_w_in, mla_q_norm=mla_q_norm,
             mla_kv_norm=mla_kv_norm, mla_w_q_up=mla_w_q_up, mla_w_kv_up=mla_w_kv_up, mla_w_o=mla_w_o,
             swa_w_in=swa_w_in, swa_sinks=swa_sinks, swa_w_o=swa_w_o, rel_bias=rel_bias,
             ffn_w_in=ffn_w_in, ffn_conv_w=ffn_conv_w, ffn_conv_b=ffn_conv_b, ffn_w_out=ffn_w_out)
    bp, tp = x_prompt.shape[:2]
    bs, ts = x_sample.shape[:2]
    n_gdn, n_ffn = state_gdn_conv.shape[0], state_ffn_conv.shape[0]
    st_prompt = dict(gdn_conv=jnp.zeros((n_gdn, bp, GDN_CONV - 1, GDN_QKV), F32),
                     gdn_S=jnp.zeros((n_gdn, bp, GDN_HEADS, GDN_DK, GDN_DV), F32),
                     ffn_conv=jnp.zeros((n_ffn, bp, FFN_CONV - 1, D_FF), F32))
    st_sample = dict(gdn_conv=state_gdn_conv, gdn_S=state_gdn_S, mla_latent=cache_mla_latent,
                     mla_krope=cache_mla_krope, swa_k=cache_swa_k, swa_v=cache_swa_v,
                     ffn_conv=state_ffn_conv)
    past_len = cache_mla_latent.shape[2]
    c_all = jnp.concatenate([c_prompt, c_sample], axis=0)
    mods = _modulation(c_all, ada_w, ada_b)
    y_prompt, sp = _trunk(x_prompt, mods[:, :bp], True, 0, st_prompt, p, min(256, tp), min(256, tp))
    y_sample, ss = _trunk(x_sample, mods[:, bp:bp + bs], False, past_len, st_sample, p, ts, ts)
    return (y_prompt, y_sample,
            sp['gdn_conv'], ss['gdn_conv'],
            sp['gdn_S'], ss['gdn_S'],
            sp['mla_latent'], ss['mla_latent'],
            sp['mla_krope'], ss['mla_krope'],
            sp['swa_k'], ss['swa_k'],
            sp['swa_v'], ss['swa_v'],
            sp['ffn_conv'], ss['ffn_conv'])
```

```python
import functools
import math

import jax
import jax.numpy as jnp
from jax import lax
from jax.experimental import pallas as pl
from jax.experimental.pallas import tpu as pltpu

F32 = jnp.float32
BF16 = jnp.bfloat16
HIGHEST = lax.Precision.HIGHEST

EPS = 1e-6
CHUNK = 64
D_MODEL = 1024

GDN_HEADS, GDN_DK, GDN_DV, GDN_CONV = 8, 128, 128, 4
GDN_QKV = GDN_HEADS * (2 * GDN_DK + GDN_DV)
GDN_CHUNK = 128

MLA_HEADS, MLA_Q_LORA, MLA_KV_LORA, MLA_NOPE, MLA_ROPE, MLA_V = 16, 384, 256, 64, 32, 64
ROPE_THETA = 10000.0
MLA_SLAB = 128
MLA_GROUP = 8

SWA_HEADS, SWA_KV_HEADS, SWA_HD, WINDOW = 16, 4, 64, 128
SWA_GROUP = SWA_HEADS // SWA_KV_HEADS
WIN_CHUNKS = WINDOW // CHUNK
REL_BUCKETS, REL_MAX_DIST = 32, 128

D_FF, FFN_CONV = 2816, 3

LANES = 128
CONV_PAD = 8
NEG = -1e30
VMEM_LIMIT = 56 * 1024 * 1024


def _cp(n_axes):
    return pltpu.CompilerParams(dimension_semantics=("arbitrary",) * n_axes, vmem_limit_bytes=VMEM_LIMIT)


def _silu(x):
    return x / (1.0 + jnp.exp(-x))


def _sigmoid(x):
    return 1.0 / (1.0 + jnp.exp(-x))


def _softplus(x):
    return jnp.maximum(x, 0.0) + jnp.log1p(jnp.exp(-jnp.abs(x)))


def _normmod(x, g, sc, sh):
    ms = jnp.mean(x * x, axis=-1, keepdims=True)
    return (x * lax.rsqrt(ms + EPS) * g) * (1.0 + sc) + sh


def _rms(x, g):
    ms = jnp.mean(x * x, axis=-1, keepdims=True)
    return x * lax.rsqrt(ms + EPS) * g


def _dot(a, b):
    return jnp.dot(a, b, preferred_element_type=F32)


def _dot_t(a, b):
    return lax.dot_general(a, b, (((1,), (1,)), ((), ())), preferred_element_type=F32)


def _const_spec(shape):
    n = len(shape)
    return pl.BlockSpec(shape, lambda *_: (0,) * n)


def _row_spec(tm, width, nt):
    return pl.BlockSpec((tm, width), lambda b, t: (b * nt + t, 0))


def _batch_vec_spec(width):
    return pl.BlockSpec((None, 1, width), lambda b, t: (b, 0, 0))


def _mod_kernel(c_ref, w_ref, b_ref, o_ref):
    cs = _silu(c_ref[...]).astype(BF16)
    o_ref[...] = _dot(cs, w_ref[...].astype(BF16)) + b_ref[...]


def _modulation(c_all, ada_w, ada_b):
    depth, d, n = ada_w.shape
    nb = c_all.shape[0]
    tn = 1536
    return pl.pallas_call(
        _mod_kernel,
        grid=(depth, n // tn),
        in_specs=[pl.BlockSpec((nb, d), lambda l, j: (0, 0)),
                  pl.BlockSpec((None, d, tn), lambda l, j: (l, 0, j)),
                  pl.BlockSpec((None, 1, tn), lambda l, j: (l, 0, j))],
        out_specs=pl.BlockSpec((None, nb, tn), lambda l, j: (l, 0, j)),
        out_shape=jax.ShapeDtypeStruct((depth, nb, n), F32),
        compiler_params=_cp(2),
        name="adaln_mod",
    )(c_all, ada_w, ada_b.reshape(depth, 1, n))


def _gdn_kernel(x_ref, sc_ref, sh_ref, gt_ref, ng_ref, hist_ref, s0_ref, wqkv_ref, wz_ref, wab_ref,
                cw_ref, alog_ref, dtb_ref, on_ref, wo_ref,
                xo_ref, nh_ref, so_ref,
                xp_scr, cv_scr, s_scr, o_scr, *, tm, rows):
    t = pl.program_id(1)
    nt = pl.num_programs(1)
    C = GDN_CHUNK
    K = GDN_CONV

    @pl.when(t == 0)
    def _():
        s_scr[...] = s0_ref[...]
        xp_scr[CONV_PAD - (K - 1):CONV_PAD, :] = hist_ref[...]

    x = x_ref[...]
    h = _normmod(x, ng_ref[...], sc_ref[...], sh_ref[...]).astype(BF16)

    qkv = _dot(h, wqkv_ref[...])
    xp_scr[CONV_PAD:CONV_PAD + tm, :] = qkv
    cw = cw_ref[...]
    conv = qkv * cw[K - 1:K]
    for j in range(K - 1):
        off = CONV_PAD - (K - 1) + j
        conv = conv + xp_scr[off:off + tm, :] * cw[j:j + 1]
    tail = qkv[tm - (K - 1):tm, :]
    xp_scr[CONV_PAD - (K - 1):CONV_PAD, :] = tail

    @pl.when(t == nt - 1)
    def _():
        nh_ref[...] = tail

    cv_scr[0:tm, :] = _silu(conv)
    if rows > tm:
        cv_scr[tm:rows, :] = jnp.zeros((rows - tm, GDN_QKV), F32)

    ab = _dot(h, wab_ref[...])
    g = -jnp.exp(alog_ref[...]) * _softplus(ab[:, :LANES] + dtb_ref[...])
    beta = _sigmoid(ab[:, LANES:])
    if rows > tm:
        zpad = jnp.zeros((rows - tm, LANES), F32)
        g = jnp.concatenate([g, zpad], axis=0)
        beta = jnp.concatenate([beta, zpad], axis=0)

    ri = lax.broadcasted_iota(jnp.int32, (rows, rows), 0)
    ci = lax.broadcasted_iota(jnp.int32, (rows, rows), 1)
    tril = jnp.where((ri >= ci) & ((ri // C) == (ci // C)), 1.0, 0.0).astype(F32)
    gc = jnp.dot(tril, g, precision=HIGHEST, preferred_element_type=F32)
    gct = gc.T

    ii = lax.broadcasted_iota(jnp.int32, (C, C), 0)
    jj = lax.broadcasted_iota(jnp.int32, (C, C), 1)
    same = []
    s = 8
    while s <= C:
        same.append((ii // s) == (jj // s))
        s *= 2
    scale = GDN_DK ** -0.5
    HK = GDN_HEADS * GDN_DK

    heads = range(GDN_HEADS)
    on = on_ref[...]
    S = [s_scr[hd] for hd in heads]
    for c in range(rows // C):
        rs = slice(c * C, (c + 1) * C)
        q, k, v, gcol, bcol, eg, kb, A, Ab, qk, N, P = ([None] * GDN_HEADS for _ in range(12))
        for hd in heads:
            qh = cv_scr[rs, hd * GDN_DK:(hd + 1) * GDN_DK]
            kh = cv_scr[rs, HK + hd * GDN_DK:HK + (hd + 1) * GDN_DK]
            v[hd] = cv_scr[rs, 2 * HK + hd * GDN_DV:2 * HK + (hd + 1) * GDN_DV]
            q[hd] = qh * lax.rsqrt(jnp.sum(qh * qh, axis=-1, keepdims=True) + EPS) * scale
            k[hd] = kh * lax.rsqrt(jnp.sum(kh * kh, axis=-1, keepdims=True) + EPS)
            gcol[hd] = gc[rs, hd:hd + 1]
            grow = gct[hd:hd + 1, rs]
            bcol[hd] = beta[rs, hd:hd + 1]
            decay = jnp.exp(jnp.where(ii >= jj, gcol[hd] - grow, NEG))
            kb[hd] = k[hd] * bcol[hd]
            kq = _dot_t(jnp.concatenate([kb[hd], q[hd]], axis=0).astype(BF16), k[hd].astype(BF16))
            A[hd] = jnp.where(ii > jj, kq[:C] * decay, 0.0)
            qk[hd] = (kq[C:] * decay).astype(BF16)
            Ab[hd] = A[hd].astype(BF16)
        for hd in heads:
            D = jnp.where(same[0], Ab[hd], 0)
            N[hd] = -jnp.where(same[0], A[hd], 0.0)
            P[hd] = _dot(D, D)
        for hd in heads:
            Pb = P[hd].astype(BF16)
            N[hd] = N[hd] + P[hd] + _dot(N[hd].astype(BF16), Pb)
            P[hd] = _dot(Pb, Pb)
        for hd in heads:
            N[hd] = N[hd] + P[hd] + _dot(N[hd].astype(BF16), P[hd].astype(BF16))
        for lvl in range(1, len(same)):
            sel = same[lvl] & jnp.logical_not(same[lvl - 1])
            X = [None] * GDN_HEADS
            for hd in heads:
                off = jnp.where(sel, Ab[hd], 0)
                X[hd] = off.astype(F32) + _dot(off, N[hd].astype(BF16))
            for hd in heads:
                N[hd] = N[hd] - X[hd] - _dot(N[hd].astype(BF16), X[hd].astype(BF16))
        w_v, w_k = [None] * GDN_HEADS, [None] * GDN_HEADS
        for hd in heads:
            eg[hd] = jnp.exp(gcol[hd])
            rhs = jnp.concatenate([v[hd] * bcol[hd], kb[hd] * eg[hd]], axis=1)
            sol = rhs + _dot(N[hd].astype(BF16), rhs.astype(BF16))
            w_v[hd], w_k[hd] = sol[:, :GDN_DV], sol[:, GDN_DV:].astype(BF16)
        Sb = [S[hd].astype(BF16) for hd in heads]
        ub = [(w_v[hd] - _dot(w_k[hd], Sb[hd])).astype(BF16) for hd in heads]
        for hd in heads:
            o = _dot((q[hd] * eg[hd]).astype(BF16), Sb[hd]) + _dot(qk[hd], ub[hd])
            gl = gcol[hd][C - 1:C, :]
            kdec = (k[hd] * jnp.exp(gl - gcol[hd])).T.astype(BF16)
            S[hd] = S[hd] * jnp.exp(gl) + _dot(kdec, ub[hd])
            o_scr[rs, hd * GDN_DV:(hd + 1) * GDN_DV] = _rms(o, on)
    for hd in heads:
        s_scr[hd] = S[hd]

    z = _dot(h, wz_ref[...])
    og = (o_scr[0:tm, :] * _silu(z)).astype(BF16)
    xo_ref[...] = x + gt_ref[...] * _dot(og, wo_ref[...])

    @pl.when(t == nt - 1)
    def _():
        so_ref[...] = s_scr[...]


def _gdn_layer(x2d, nb, T, sc, sh, gt, ng, hist, s0, w_in, conv_w, a_log, dt_bias, o_norm, w_o, tm):
    nt = T // tm
    rows = max(tm, GDN_CHUNK)
    H, dv = GDN_HEADS, GDN_DV
    wqkv = w_in[:, :GDN_QKV].astype(BF16)
    wz = w_in[:, GDN_QKV:GDN_QKV + H * dv].astype(BF16)
    wa = w_in[:, GDN_QKV + H * dv:GDN_QKV + H * dv + H]
    wb = w_in[:, GDN_QKV + H * dv + H:]
    lane_pad = ((0, 0), (0, LANES - H))
    wab = jnp.concatenate([jnp.pad(wa, lane_pad), jnp.pad(wb, lane_pad)], axis=1).astype(BF16)
    alog = jnp.pad(a_log.reshape(1, H), lane_pad)
    dtb = jnp.pad(dt_bias.reshape(1, H), lane_pad)
    kern = functools.partial(_gdn_kernel, tm=tm, rows=rows)
    return pl.pallas_call(
        kern,
        grid=(nb, nt),
        in_specs=[_row_spec(tm, D_MODEL, nt), _batch_vec_spec(D_MODEL), _batch_vec_spec(D_MODEL),
                  _batch_vec_spec(D_MODEL), _const_spec((1, D_MODEL)),
                  pl.BlockSpec((None, GDN_CONV - 1, GDN_QKV), lambda b, t: (b, 0, 0)),
                  pl.BlockSpec((None, H, GDN_DK, dv), lambda b, t: (b, 0, 0, 0)),
                  _const_spec((D_MODEL, GDN_QKV)), _const_spec((D_MODEL, H * dv)),
                  _const_spec((D_MODEL, 2 * LANES)), _const_spec((GDN_CONV, GDN_QKV)),
                  _const_spec((1, LANES)), _const_spec((1, LANES)), _const_spec((1, dv)),
                  _const_spec((H * dv, D_MODEL))],
        out_specs=[_row_spec(tm, D_MODEL, nt),
                   pl.BlockSpec((None, GDN_CONV - 1, GDN_QKV), lambda b, t: (b, 0, 0)),
                   pl.BlockSpec((None, H, GDN_DK, dv), lambda b, t: (b, 0, 0, 0))],
        out_shape=[jax.ShapeDtypeStruct((nb * T, D_MODEL), F32),
                   jax.ShapeDtypeStruct((nb, GDN_CONV - 1, GDN_QKV), F32),
                   jax.ShapeDtypeStruct((nb, H, GDN_DK, dv), F32)],
        scratch_shapes=[pltpu.VMEM((CONV_PAD + tm, GDN_QKV), F32),
                        pltpu.VMEM((rows, GDN_QKV), F32),
                        pltpu.VMEM((H, GDN_DK, dv), F32),
                        pltpu.VMEM((rows, H * dv), F32)],
        compiler_params=_cp(2),
        name="gdn_layer",
    )(x2d, sc, sh, gt, ng.reshape(1, D_MODEL), hist, s0, wqkv, wz, wab, conv_w, alog, dtb,
      o_norm.reshape(1, dv), w_o.astype(BF16))


def _ffn_kernel(x_ref, sc_ref, sh_ref, gt_ref, ng_ref, hist_ref, wg_ref, wu_ref, cw_ref, cb_ref, wo_ref, fn_ref,
                xo_ref, nh_ref, xp_scr, *, tm, final):
    t = pl.program_id(1)
    nt = pl.num_programs(1)
    K = FFN_CONV

    @pl.when(t == 0)
    def _():
        xp_scr[CONV_PAD - (K - 1):CONV_PAD, :] = hist_ref[...]

    x = x_ref[...]
    h = _normmod(x, ng_ref[...], sc_ref[...], sh_ref[...]).astype(BF16)
    gate = _dot(h, wg_ref[...])
    xp_scr[CONV_PAD:CONV_PAD + tm, :] = gate
    cw = cw_ref[...]
    conv = gate * cw[K - 1:K]
    for j in range(K - 1):
        off = CONV_PAD - (K - 1) + j
        conv = conv + xp_scr[off:off + tm, :] * cw[j:j + 1]
    tail = gate[tm - (K - 1):tm, :]
    xp_scr[CONV_PAD - (K - 1):CONV_PAD, :] = tail

    @pl.when(t == nt - 1)
    def _():
        nh_ref[...] = tail

    up = _dot(h, wu_ref[...])
    act = (_silu(conv + cb_ref[...]) * up).astype(BF16)
    y = x + gt_ref[...] * _dot(act, wo_ref[...])
    if final:
        y = _rms(y, fn_ref[...])
    xo_ref[...] = y


def _ffn_layer(x2d, nb, T, sc, sh, gt, ng, hist, w_in, conv_w, conv_b, w_out, final_g, final, tm):
    nt = T // tm
    wg = w_in[:, :D_FF].astype(BF16)
    wu = w_in[:, D_FF:].astype(BF16)
    kern = functools.partial(_ffn_kernel, tm=tm, final=final)
    return pl.pallas_call(
        kern,
        grid=(nb, nt),
        in_specs=[_row_spec(tm, D_MODEL, nt), _batch_vec_spec(D_MODEL), _batch_vec_spec(D_MODEL),
                  _batch_vec_spec(D_MODEL), _const_spec((1, D_MODEL)),
                  pl.BlockSpec((None, FFN_CONV - 1, D_FF), lambda b, t: (b, 0, 0)),
                  _const_spec((D_MODEL, D_FF)), _const_spec((D_MODEL, D_FF)),
                  _const_spec((FFN_CONV, D_FF)), _const_spec((1, D_FF)),
                  _const_spec((D_FF, D_MODEL)), _const_spec((1, D_MODEL))],
        out_specs=[_row_spec(tm, D_MODEL, nt),
                   pl.BlockSpec((None, FFN_CONV - 1, D_FF), lambda b, t: (b, 0, 0))],
        out_shape=[jax.ShapeDtypeStruct((nb * T, D_MODEL), F32),
                   jax.ShapeDtypeStruct((nb, FFN_CONV - 1, D_FF), F32)],
        scratch_shapes=[pltpu.VMEM((CONV_PAD + tm, D_FF), F32)],
        compiler_params=_cp(2),
        name="conv_ffn_layer",
    )(x2d, sc, sh, gt, ng.reshape(1, D_MODEL), hist, wg, wu, conv_w, conv_b.reshape(1, D_FF),
      w_out.astype(BF16), final_g.reshape(1, D_MODEL))


def _proj_res_kernel(a_ref, w_ref, x_ref, gt_ref, o_ref):
    o_ref[...] = x_ref[...] + gt_ref[...] * _dot(a_ref[...], w_ref[...])


def _proj_res(a2d, w, x2d, gt, nb, T, tm):
    nt = T // tm
    kdim = a2d.shape[1]
    return pl.pallas_call(
        _proj_res_kernel,
        grid=(nb, nt),
        in_specs=[_row_spec(tm, kdim, nt), _const_spec((kdim, D_MODEL)), _row_spec(tm, D_MODEL, nt),
                  _batch_vec_spec(D_MODEL)],
        out_specs=_row_spec(tm, D_MODEL, nt),
        out_shape=jax.ShapeDtypeStruct((nb * T, D_MODEL), F32),
        compiler_params=_cp(2),
        name="attn_out_proj",
    )(a2d, w.astype(BF16), x2d, gt)


def _mla_q_kernel(x_ref, sc_ref, sh_ref, ng_ref, wq_ref, wkv_ref, wr_ref, qn_ref, kvn_ref, wu1_ref, wu2_ref,
                  cos_ref, sin_ref, q_ref, ckv_ref, kr_ref):
    h = _normmod(x_ref[...], ng_ref[...], sc_ref[...], sh_ref[...]).astype(BF16)
    cq = _rms(_dot(h, wq_ref[...]), qn_ref[...]).astype(BF16)
    ckv_ref[...] = _rms(_dot(h, wkv_ref[...]), kvn_ref[...])
    cs = cos_ref[...]
    sn = sin_ref[...]
    r = _dot(h, wr_ref[...])
    kr_ref[...] = r[:, :MLA_SLAB] * cs + r[:, MLA_SLAB:] * sn
    q1 = _dot(cq, wu1_ref[...])
    q2 = _dot(cq, wu2_ref[...])
    for hd in range(MLA_HEADS):
        sl = slice(hd * MLA_SLAB, (hd + 1) * MLA_SLAB)
        q_ref[:, sl] = (q1[:, sl] * cs + q2[:, sl] * sn).astype(BF16)


def _rope_tables(pos):
    half = MLA_ROPE // 2
    inv = ROPE_THETA ** (-jnp.arange(half, dtype=F32) / half)
    ang = pos.astype(F32)[:, None] * inv[None, :]
    cos, sin = jnp.cos(ang), jnp.sin(ang)
    n = pos.shape[0]
    pad = jnp.zeros((n, MLA_SLAB - MLA_NOPE - MLA_ROPE), F32)
    cs = jnp.concatenate([jnp.ones((n, MLA_NOPE), F32), cos, cos, pad], axis=1)
    sn = jnp.concatenate([jnp.zeros((n, MLA_NOPE), F32), -sin, sin, pad], axis=1)
    return cs, sn


def _slab_weights(w_q_up, w_kv_up, w_r):
    half = MLA_ROPE // 2
    hq = MLA_NOPE + MLA_ROPE
    padw = MLA_SLAB - hq
    wq = w_q_up.reshape(MLA_Q_LORA, MLA_HEADS, hq)
    wq_n, wq_r = wq[..., :MLA_NOPE], wq[..., MLA_NOPE:]
    wq_sw = jnp.concatenate([wq_r[..., half:], wq_r[..., :half]], axis=-1)
    zq = jnp.zeros((MLA_Q_LORA, MLA_HEADS, padw), F32)
    wu1 = jnp.concatenate([wq_n, wq_r, zq], axis=-1).reshape(MLA_Q_LORA, MLA_HEADS * MLA_SLAB)
    wu2 = jnp.concatenate([jnp.zeros_like(wq_n), wq_sw, zq], axis=-1).reshape(MLA_Q_LORA, MLA_HEADS * MLA_SLAB)
    wkv = w_kv_up.reshape(MLA_KV_LORA, MLA_HEADS, MLA_NOPE + MLA_V)
    wk = jnp.concatenate([wkv[..., :MLA_NOPE], jnp.zeros((MLA_KV_LORA, MLA_HEADS, MLA_SLAB - MLA_NOPE), F32)],
                         axis=-1).reshape(MLA_KV_LORA, MLA_HEADS * MLA_SLAB)
    wv = wkv[..., MLA_NOPE:].reshape(MLA_KV_LORA, MLA_HEADS * MLA_V)
    d = w_r.shape[0]
    wr_sw = jnp.concatenate([w_r[:, half:], w_r[:, :half]], axis=1)
    zl = jnp.zeros((d, MLA_NOPE), F32)
    zr = jnp.zeros((d, padw), F32)
    wr2 = jnp.concatenate([zl, w_r, zr, zl, wr_sw, zr], axis=1)
    return wu1.astype(BF16), wu2.astype(BF16), wk.astype(BF16), wv.astype(BF16), wr2.astype(BF16)


def _mla_q(x2d, nb, T, sc, sh, ng, wq, wkv, wr2, qn, kvn, wu1, wu2, cs, sn, tm):
    nt = T // tm
    qw = MLA_HEADS * MLA_SLAB
    return pl.pallas_call(
        _mla_q_kernel,
        grid=(nb, nt),
        in_specs=[_row_spec(tm, D_MODEL, nt), _batch_vec_spec(D_MODEL), _batch_vec_spec(D_MODEL),
                  _const_spec((1, D_MODEL)), _const_spec((D_MODEL, MLA_Q_LORA)),
                  _const_spec((D_MODEL, MLA_KV_LORA)), _const_spec((D_MODEL, 2 * MLA_SLAB)),
                  _const_spec((1, MLA_Q_LORA)), _const_spec((1, MLA_KV_LORA)),
                  _const_spec((MLA_Q_LORA, qw)), _const_spec((MLA_Q_LORA, qw)),
                  pl.BlockSpec((tm, MLA_SLAB), lambda b, t: (t, 0)),
                  pl.BlockSpec((tm, MLA_SLAB), lambda b, t: (t, 0))],
        out_specs=[_row_spec(tm, qw, nt), _row_spec(tm, MLA_KV_LORA, nt), _row_spec(tm, MLA_SLAB, nt)],
        out_shape=[jax.ShapeDtypeStruct((nb * T, qw), BF16),
                   jax.ShapeDtypeStruct((nb * T, MLA_KV_LORA), F32),
                   jax.ShapeDtypeStruct((nb * T, MLA_SLAB), F32)],
        compiler_params=_cp(2),
        name="mla_q_proj",
    )(x2d, sc, sh, ng.reshape(1, D_MODEL), wq, wkv, wr2, qn.reshape(1, -1), kvn.reshape(1, -1), wu1, wu2, cs, sn)


def _mla_kv_kernel(ckv_ref, kr_ref, wk_ref, wvt_ref, k_ref, vt_ref):
    c = ckv_ref[...].astype(BF16)
    kn = _dot(c, wk_ref[...])
    kr = kr_ref[...]
    for hd in range(MLA_HEADS):
        sl = slice(hd * MLA_SLAB, (hd + 1) * MLA_SLAB)
        k_ref[:, sl] = (kn[:, sl] + kr).astype(BF16)
    vt_ref[...] = _dot_t(wvt_ref[...], c).astype(BF16)


def _mla_kv(ckv2d, kr2d, wk, wvt, nb, Tk, tm):
    nt = Tk // tm
    kw = MLA_HEADS * MLA_SLAB
    vw = MLA_HEADS * MLA_V
    return pl.pallas_call(
        _mla_kv_kernel,
        grid=(nb, nt),
        in_specs=[_row_spec(tm, MLA_KV_LORA, nt), _row_spec(tm, MLA_SLAB, nt),
                  _const_spec((MLA_KV_LORA, kw)), _const_spec((vw, MLA_KV_LORA))],
        out_specs=[_row_spec(tm, kw, nt), pl.BlockSpec((None, vw, tm), lambda b, t: (b, 0, t))],
        out_shape=[jax.ShapeDtypeStruct((nb * Tk, kw), BF16), jax.ShapeDtypeStruct((nb, vw, Tk), BF16)],
        compiler_params=_cp(2),
        name="mla_kv_up",
    )(ckv2d, kr2d, wk, wvt)


def _mla_attn_kernel(q_ref, k_ref, vt_ref, o_ref, *, tq, tk, causal, n_kv_full):
    i = pl.program_id(2)
    scale = (MLA_NOPE + MLA_ROPE) ** -0.5
    heads = range(MLA_GROUP)
    qs = [q_ref[:, hh * MLA_SLAB:(hh + 1) * MLA_SLAB] for hh in heads]

    def tile(off, carry, masked):
        if masked:
            krow = lax.broadcasted_iota(jnp.int32, (tk, tq), 0)
            qcol = lax.broadcasted_iota(jnp.int32, (tk, tq), 1)
            keep = (krow // CHUNK) <= (qcol // CHUNK)
        s = [_dot_t(k_ref[pl.ds(off, tk), hh * MLA_SLAB:(hh + 1) * MLA_SLAB], qs[hh]) * scale for hh in heads]
        if masked:
            s = [jnp.where(keep, s[hh], NEG) for hh in heads]
        m_new = [jnp.maximum(carry[hh][0], jnp.max(s[hh], axis=0, keepdims=True)) for hh in heads]
        p = [jnp.exp(s[hh] - m_new[hh]) for hh in heads]
        out = []
        for hh in heads:
            m, l, acc = carry[hh]
            alpha = jnp.exp(m - m_new[hh])
            l = alpha * l + jnp.sum(p[hh], axis=0, keepdims=True)
            vt = vt_ref[hh * MLA_V:(hh + 1) * MLA_V, pl.ds(off, tk)]
            acc = alpha * acc + _dot(vt, p[hh].astype(BF16))
            out.append((m_new[hh], l, acc))
        return tuple(out)

    init = tuple((jnp.full((1, tq), NEG, F32), jnp.zeros((1, tq), F32), jnp.zeros((MLA_V, tq), F32))
                 for _ in heads)
    n_open = i if causal else n_kv_full
    res = lax.fori_loop(0, n_open, lambda j, c: tile(pl.multiple_of(j * tk, tk), c, False), init)
    if causal:
        res = tile(pl.multiple_of(i * tq, tq), res, True)
    ot = jnp.concatenate([res[hh][2] / res[hh][1] for hh in heads], axis=0).astype(BF16)
    r = lax.broadcasted_iota(jnp.int32, (tq, tq), 0)
    c = lax.broadcasted_iota(jnp.int32, (tq, tq), 1)
    eye = jnp.where(r == c, 1.0, 0.0).astype(BF16)
    o_ref[...] = _dot_t(eye, ot).astype(BF16)


def _mla_attn(q3, k3, vt3, tq, tk, causal):
    nb, Tq, _ = q3.shape
    Tk = k3.shape[1]
    G = MLA_GROUP
    kern = functools.partial(_mla_attn_kernel, tq=tq, tk=tk, causal=causal, n_kv_full=Tk // tk)
    return pl.pallas_call(
        kern,
        grid=(nb, MLA_HEADS // G, Tq // tq),
        in_specs=[pl.BlockSpec((None, tq, G * MLA_SLAB), lambda b, p, i: (b, i, p)),
                  pl.BlockSpec((None, Tk, G * MLA_SLAB), lambda b, p, i: (b, 0, p)),
                  pl.BlockSpec((None, G * MLA_V, Tk), lambda b, p, i: (b, p, 0))],
        out_specs=pl.BlockSpec((None, tq, G * MLA_V), lambda b, p, i: (b, i, p)),
        out_shape=jax.ShapeDtypeStruct((nb, Tq, MLA_HEADS * MLA_V), BF16),
        compiler_params=_cp(3),
        name="mla_attention",
    )(q3, k3, vt3)


def _swa_proj_kernel(x_ref, sc_ref, sh_ref, ng_ref, w_ref, q_ref, kv_ref, last_ref, *, tm, keep):
    h = _normmod(x_ref[...], ng_ref[...], sc_ref[...], sh_ref[...]).astype(BF16)
    proj = _dot(h, w_ref[...])
    nq = SWA_HEADS * SWA_HD
    q_ref[...] = proj[:, :nq].astype(BF16)
    kv = proj[:, nq:]
    kv_ref[...] = kv.astype(BF16)
    last_ref[...] = kv[tm - keep:, :]


def _swa_proj(x2d, nb, T, sc, sh, ng, w_in, tm):
    nt = T // tm
    keep = min(WINDOW, T)
    nq = SWA_HEADS * SWA_HD
    nkv = 2 * SWA_KV_HEADS * SWA_HD
    kern = functools.partial(_swa_proj_kernel, tm=tm, keep=keep)
    return pl.pallas_call(
        kern,
        grid=(nb, nt),
        in_specs=[_row_spec(tm, D_MODEL, nt), _batch_vec_spec(D_MODEL), _batch_vec_spec(D_MODEL),
                  _const_spec((1, D_MODEL)), _const_spec((D_MODEL, nq + nkv))],
        out_specs=[_row_spec(tm, nq, nt), _row_spec(tm, nkv, nt),
                   pl.BlockSpec((None, keep, nkv), lambda b, t: (b, 0, 0))],
        out_shape=[jax.ShapeDtypeStruct((nb * T, nq), BF16), jax.ShapeDtypeStruct((nb * T, nkv), BF16),
                   jax.ShapeDtypeStruct((nb, keep, nkv), F32)],
        compiler_params=_cp(2),
        name="swa_proj",
    )(x2d, sc, sh, ng.reshape(1, D_MODEL), w_in.astype(BF16))


def _t5_bias_kernel(bucket_ref, table_ref, o_ref):
    hd = pl.program_id(0)
    bucket = bucket_ref[...]
    acc = jnp.full(bucket.shape, NEG, F32)
    for b in range(REL_BUCKETS):
        acc = jnp.where(bucket == b, table_ref[b, hd], acc)
    o_ref[...] = acc


def _t5_bias(rel_bias, q_pos, k_pos, valid):
    n = q_pos[:, None] - k_pos[None, :]
    half = REL_BUCKETS // 2
    exact = half // 2
    side = jnp.where(n < 0, half, 0)
    n = jnp.abs(n)
    log_b = exact + (jnp.log(jnp.maximum(n, 1).astype(F32) / exact)
                     / math.log(REL_MAX_DIST / exact) * (half - exact)).astype(jnp.int32)
    bucket = side + jnp.where(n < exact, n, jnp.minimum(log_b, half - 1))
    bucket = jnp.where(valid, bucket, -1).astype(jnp.int32)
    tq, tk = bucket.shape
    return pl.pallas_call(
        _t5_bias_kernel,
        grid=(SWA_HEADS,),
        in_specs=[pl.BlockSpec((tq, tk), lambda h: (0, 0)),
                  pl.BlockSpec(memory_space=pltpu.SMEM)],
        out_specs=pl.BlockSpec((None, tq, tk), lambda h: (h, 0, 0)),
        out_shape=jax.ShapeDtypeStruct((SWA_HEADS, tq, tk), F32),
        compiler_params=_cp(1),
        name="t5_bias",
    )(bucket, rel_bias)


def _swa_attn_kernel(q_ref, kv_ref, bias_ref, sink_ref, o_ref, *, tq, lk, npad):
    i = pl.program_id(1)
    off = pl.multiple_of(i * tq, tq)
    kvw = kv_ref[pl.ds(off, lk), :]
    nk = SWA_KV_HEADS * SWA_HD
    q = q_ref[...]
    if npad:
        col = lax.broadcasted_iota(jnp.int32, (tq, lk), 1) + off
        real = col >= npad
    outs = []
    for hd in range(SWA_HEADS):
        kvh = hd // SWA_GROUP
        qh = q[:, hd * SWA_HD:(hd + 1) * SWA_HD]
        kh = kvw[:, kvh * SWA_HD:(kvh + 1) * SWA_HD]
        vh = kvw[:, nk + kvh * SWA_HD:nk + (kvh + 1) * SWA_HD]
        s = _dot_t(qh, kh) * (SWA_HD ** -0.5) + bias_ref[hd]
        if npad:
            s = jnp.where(real, s, NEG)
        sink = sink_ref[hd]
        m = jnp.maximum(jnp.max(s, axis=-1, keepdims=True), sink)
        p = jnp.exp(s - m)
        p = p / (jnp.sum(p, axis=-1, keepdims=True) + jnp.exp(sink - m))
        outs.append(_dot(p.astype(BF16), vh))
    o_ref[...] = jnp.concatenate(outs, axis=-1).astype(BF16)


def _swa_attn(q3, kv3, bias, sinks, tq, lk, npad):
    nb, Tq, nq = q3.shape
    Tk, nkv = kv3.shape[1:]
    kern = functools.partial(_swa_attn_kernel, tq=tq, lk=lk, npad=npad)
    return pl.pallas_call(
        kern,
        grid=(nb, Tq // tq),
        in_specs=[pl.BlockSpec((None, tq, nq), lambda b, i: (b, i, 0)),
                  pl.BlockSpec((None, Tk, nkv), lambda b, i: (b, 0, 0)),
                  _const_spec((SWA_HEADS, tq, lk)),
                  pl.BlockSpec(memory_space=pltpu.SMEM)],
        out_specs=pl.BlockSpec((None, tq, nq), lambda b, i: (b, i, 0)),
        out_shape=jax.ShapeDtypeStruct((nb, Tq, nq), BF16),
        compiler_params=_cp(2),
        name="swa_attention",
    )(q3, kv3, bias, sinks)


def _trunk(x, mods, prompt, past_len, st, p, tm, ffn_tm):
    B, T, _ = x.shape
    x2d = x.reshape(B * T, D_MODEL)
    pos = jnp.arange(T, dtype=jnp.int32) + (0 if prompt else past_len)
    new = dict(gdn_conv=[], gdn_S=[], mla_latent=[], mla_krope=[], swa_k=[], swa_v=[], ffn_conv=[])
    depth = p['ada_w'].shape[0]
    for layer in range(depth):
        kind, slot = layer % 3, layer // 3
        sh1, sc1, g1, sh2, sc2, g2 = [mods[layer, :, j * D_MODEL:(j + 1) * D_MODEL][:, None, :] for j in range(6)]
        ng = p['norm1'][layer]
        if kind == 0:
            x2d, conv_h, S = _gdn_layer(x2d, B, T, sc1, sh1, g1, ng, st['gdn_conv'][slot], st['gdn_S'][slot],
                                        p['gdn_w_in'][slot], p['gdn_conv_w'][slot], p['gdn_a_log'][slot],
                                        p['gdn_dt_bias'][slot], p['gdn_o_norm'][slot], p['gdn_w_o'][slot], tm)
            new['gdn_conv'].append(conv_h)
            new['gdn_S'].append(S)
        elif kind == 1:
            w_in = p['mla_w_in'][slot]
            wq = w_in[:, :MLA_Q_LORA].astype(BF16)
            wkv = w_in[:, MLA_Q_LORA:MLA_Q_LORA + MLA_KV_LORA].astype(BF16)
            wu1, wu2, wk, wv, wr2 = _slab_weights(p['mla_w_q_up'][slot], p['mla_w_kv_up'][slot],
                                                  w_in[:, MLA_Q_LORA + MLA_KV_LORA:])
            cs, sn = _rope_tables(pos)
            qcat, ckv, kr = _mla_q(x2d, B, T, sc1, sh1, ng, wq, wkv, wr2, p['mla_q_norm'][slot],
                                   p['mla_kv_norm'][slot], wu1, wu2, cs, sn, tm)
            ckv3 = ckv.reshape(B, T, MLA_KV_LORA)
            kr3 = kr.reshape(B, T, MLA_SLAB)
            if prompt:
                ckv_all, kr_all, Tk = ckv, kr, T
                kv_tm, tq, tk = tm, tm, tm
            else:
                assert past_len % CHUNK + T <= CHUNK
                cache_kr = jnp.pad(st['mla_krope'][slot].astype(F32),
                                   ((0, 0), (0, 0), (MLA_NOPE, MLA_SLAB - MLA_NOPE - MLA_ROPE)))
                Tk = past_len + T
                ckv_all = jnp.concatenate([st['mla_latent'][slot].astype(F32), ckv3], axis=1).reshape(B * Tk, -1)
                kr_all = jnp.concatenate([cache_kr, kr3], axis=1).reshape(B * Tk, MLA_SLAB)
                kv_tm, tq, tk = Tk, T, Tk
            kcat, vt = _mla_kv(ckv_all, kr_all, wk, wv.T, B, Tk, kv_tm)
            o = _mla_attn(qcat.reshape(B, T, -1), kcat.reshape(B, Tk, -1), vt, tq, tk, prompt)
            x2d = _proj_res(o.reshape(B * T, -1), p['mla_w_o'][slot], x2d, g1, B, T, tm)
            new['mla_latent'].append(ckv3)
            new['mla_krope'].append(kr3[:, :, MLA_NOPE:MLA_NOPE + MLA_ROPE])
        else:
            q, kvb, kv_last = _swa_proj(x2d, B, T, sc1, sh1, ng, p['swa_w_in'][slot], tm)
            nk = SWA_KV_HEADS * SWA_HD
            if prompt:
                tq = 2 * CHUNK
                lk = tq + WINDOW
                rq = WINDOW + jnp.arange(tq)
                rk = jnp.arange(lk)
                d = rq[:, None] // CHUNK - rk[None, :] // CHUNK
                bias = _t5_bias(p['rel_bias'], rq, rk, (d >= 0) & (d <= WIN_CHUNKS))
                kv3 = jnp.pad(kvb.reshape(B, T, 2 * nk), ((0, 0), (WINDOW, 0), (0, 0)))
                heads = _swa_attn(q.reshape(B, T, -1), kv3, bias, p['swa_sinks'][slot], tq, lk, WINDOW)
                new_k, new_v = kv_last[:, :, :nk], kv_last[:, :, nk:]
            else:
                win = st['swa_k'].shape[2]
                ck = st['swa_k'][slot].reshape(B, win, nk).astype(F32)
                cv = st['swa_v'][slot].reshape(B, win, nk).astype(F32)
                k_all = jnp.concatenate([ck, kv_last[:, :, :nk]], axis=1)
                v_all = jnp.concatenate([cv, kv_last[:, :, nk:]], axis=1)
                k_pos = jnp.concatenate([past_len - win + jnp.arange(win, dtype=jnp.int32), pos])
                d = (pos // CHUNK)[:, None] - (k_pos // CHUNK)[None, :]
                bias = _t5_bias(p['rel_bias'], pos, k_pos, (d >= 0) & (d <= WIN_CHUNKS))
                kv3 = jnp.concatenate([k_all, v_all], axis=-1).astype(BF16)
                heads = _swa_attn(q.reshape(B, T, -1), kv3, bias, p['swa_sinks'][slot], T, win + T, 0)
                new_k, new_v = k_all[:, -win:], v_all[:, -win:]
            x2d = _proj_res(heads.reshape(B * T, -1), p['swa_w_o'][slot], x2d, g1, B, T, tm)
            new['swa_k'].append(new_k.reshape(B, -1, SWA_KV_HEADS, SWA_HD))
            new['swa_v'].append(new_v.reshape(B, -1, SWA_KV_HEADS, SWA_HD))
        x2d, f_hist = _ffn_layer(x2d, B, T, sc2, sh2, g2, p['norm2'][layer], st['ffn_conv'][layer],
                                 p['ffn_w_in'][layer], p['ffn_conv_w'][layer], p['ffn_conv_b'][layer],
                                 p['ffn_w_out'][layer], p['final_norm'], layer == depth - 1, ffn_tm)
        new['ffn_conv'].append(f_hist)
    return x2d.reshape(B, T, D_MODEL), {name: jnp.stack(rows) for name, rows in new.items()}


def kernel(x_prompt, x_sample, c_prompt, c_sample, state_gdn_conv, state_gdn_S, cache_mla_latent, cache_mla_krope, cache_swa_k, cache_swa_v, state_ffn_conv, ada_w, ada_b, norm1, norm2, final_norm, gdn_w_in, gdn_conv_w, gdn_a_log, gdn_dt_bias, gdn_o_norm, gdn_w_o, mla_w_in, mla_q_norm, mla_kv_norm, mla_w_q_up, mla_w_kv_up, mla_w_o, swa_w_in, swa_sinks, swa_w_o, rel_bias, ffn_w_in, ffn_conv_w, ffn_conv_b, ffn_w_out):
    p = dict(ada_w=ada_w, ada_b=ada_b, norm1=norm1, norm2=norm2, final_norm=final_norm,
             gdn_w_in=gdn_w_in, gdn_conv_w=gdn_conv_w, gdn_a_log=gdn_a_log, gdn_dt_bias=gdn_dt_bias,
             gdn_o_norm=gdn_o_norm, gdn_w_o=gdn_w_o, mla_w_in=mla_w_in, mla_q_norm=mla_q_norm,
             mla_kv_norm=mla_kv_norm, mla_w_q_up=mla_w_q_up, mla_w_kv_up=mla_w_kv_up, mla_w_o=mla_w_o,
             swa_w_in=swa_w_in, swa_sinks=swa_sinks, swa_w_o=swa_w_o, rel_bias=rel_bias,
             ffn_w_in=ffn_w_in, ffn_conv_w=ffn_conv_w, ffn_conv_b=ffn_conv_b, ffn_w_out=ffn_w_out)
    bp, tp = x_prompt.shape[:2]
    bs, ts = x_sample.shape[:2]
    n_gdn, n_ffn = state_gdn_conv.shape[0], state_ffn_conv.shape[0]
    st_prompt = dict(gdn_conv=jnp.zeros((n_gdn, bp, GDN_CONV - 1, GDN_QKV), F32),
                     gdn_S=jnp.zeros((n_gdn, bp, GDN_HEADS, GDN_DK, GDN_DV), F32),
                     ffn_conv=jnp.zeros((n_ffn, bp, FFN_CONV - 1, D_FF), F32))
    st_sample = dict(gdn_conv=state_gdn_conv, gdn_S=state_gdn_S, mla_latent=cache_mla_latent,
                     mla_krope=cache_mla_krope, swa_k=cache_swa_k, swa_v=cache_swa_v,
                     ffn_conv=state_ffn_conv)
    past_len = cache_mla_latent.shape[2]
    c_all = jnp.concatenate([c_prompt, c_sample], axis=0)
    mods = _modulation(c_all, ada_w, ada_b)
    y_prompt, sp = _trunk(x_prompt, mods[:, :bp], True, 0, st_prompt, p, min(256, tp), min(256, tp))
    y_sample, ss = _trunk(x_sample, mods[:, bp:bp + bs], False, past_len, st_sample, p, ts, ts)
    return (y_prompt, y_sample,
            sp['gdn_conv'], ss['gdn_conv'],
            sp['gdn_S'], ss['gdn_S'],
            sp['mla_latent'], ss['mla_latent'],
            sp['mla_krope'], ss['mla_krope'],
            sp['swa_k'], ss['swa_k'],
            sp['swa_v'], ss['swa_v'],
            sp['ffn_conv'], ss['ffn_conv'])
```

```python
import functools
import math

import jax
import jax.numpy as jnp
from jax import lax
from jax.experimental import pallas as pl
from jax.experimental.pallas import tpu as pltpu

F32 = jnp.float32
BF16 = jnp.bfloat16
HIGHEST = lax.Precision.HIGHEST

EPS = 1e-6
CHUNK = 64
D_MODEL = 1024

GDN_HEADS, GDN_DK, GDN_DV, GDN_CONV = 8, 128, 128, 4
GDN_QKV = GDN_HEADS * (2 * GDN_DK + GDN_DV)
GDN_CHUNK = 128

MLA_HEADS, MLA_Q_LORA, MLA_KV_LORA, MLA_NOPE, MLA_ROPE, MLA_V = 16, 384, 256, 64, 32, 64
ROPE_THETA = 10000.0
MLA_SLAB = 128
MLA_GROUP = 8

SWA_HEADS, SWA_KV_HEADS, SWA_HD, WINDOW = 16, 4, 64, 128
SWA_GROUP = SWA_HEADS // SWA_KV_HEADS
WIN_CHUNKS = WINDOW // CHUNK
REL_BUCKETS, REL_MAX_DIST = 32, 128

D_FF, FFN_CONV = 2816, 3

LANES = 128
CONV_PAD = 8
NEG = -1e30
VMEM_LIMIT = 56 * 1024 * 1024


def _cp(n_axes):
    return pltpu.CompilerParams(dimension_semantics=("arbitrary",) * n_axes, vmem_limit_bytes=VMEM_LIMIT)


def _silu(x):
    return x / (1.0 + jnp.exp(-x))


def _sigmoid(x):
    return 1.0 / (1.0 + jnp.exp(-x))


def _softplus(x):
    return jnp.maximum(x, 0.0) + jnp.log1p(jnp.exp(-jnp.abs(x)))


def _normmod(x, g, sc, sh):
    ms = jnp.mean(x * x, axis=-1, keepdims=True)
    return (x * lax.rsqrt(ms + EPS) * g) * (1.0 + sc) + sh


def _rms(x, g):
    ms = jnp.mean(x * x, axis=-1, keepdims=True)
    return x * lax.rsqrt(ms + EPS) * g


def _dot(a, b):
    return jnp.dot(a, b, preferred_element_type=F32)


def _dot_t(a, b):
    return lax.dot_general(a, b, (((1,), (1,)), ((), ())), preferred_element_type=F32)


def _const_spec(shape):
    n = len(shape)
    return pl.BlockSpec(shape, lambda *_: (0,) * n)


def _resident_spec(shape):
    n = len(shape)
    return pl.BlockSpec(shape, lambda *_: (0,) * n, pipeline_mode=pl.Buffered(1))


def _row_spec(tm, width, nt):
    return pl.BlockSpec((tm, width), lambda b, t: (b * nt + t, 0))


def _batch_vec_spec(width):
    return pl.BlockSpec((None, 1, width), lambda b, t: (b, 0, 0))


def _mod_kernel(c_ref, w_ref, b_ref, o_ref):
    cs = _silu(c_ref[...]).astype(BF16)
    o_ref[...] = _dot(cs, w_ref[...].astype(BF16)) + b_ref[...]


def _modulation(c_all, ada_w, ada_b):
    depth, d, n = ada_w.shape
    nb = c_all.shape[0]
    tn = 1536
    return pl.pallas_call(
        _mod_kernel,
        grid=(depth, n // tn),
        in_specs=[pl.BlockSpec((nb, d), lambda l, j: (0, 0)),
                  pl.BlockSpec((None, d, tn), lambda l, j: (l, 0, j)),
                  pl.BlockSpec((None, 1, tn), lambda l, j: (l, 0, j))],
        out_specs=pl.BlockSpec((None, nb, tn), lambda l, j: (l, 0, j)),
        out_shape=jax.ShapeDtypeStruct((depth, nb, n), F32),
        compiler_params=_cp(2),
        name="adaln_mod",
    )(c_all, ada_w, ada_b.reshape(depth, 1, n))


def _gdn_kernel(x_ref, sc_ref, sh_ref, gt_ref, ng_ref, hist_ref, s0_ref, wqkv_ref, wz_ref, wab_ref,
                cw_ref, alog_ref, dtb_ref, on_ref, wo_ref,
                xo_ref, nh_ref, so_ref,
                xp_scr, cv_scr, s_scr, o_scr, *, tm, rows):
    t = pl.program_id(1)
    nt = pl.num_programs(1)
    C = GDN_CHUNK
    K = GDN_CONV

    @pl.when(t == 0)
    def _():
        s_scr[...] = s0_ref[...]
        xp_scr[CONV_PAD - (K - 1):CONV_PAD, :] = hist_ref[...]

    x = x_ref[...]
    h = _normmod(x, ng_ref[...], sc_ref[...], sh_ref[...]).astype(BF16)

    qkv = _dot(h, wqkv_ref[...])
    xp_scr[CONV_PAD:CONV_PAD + tm, :] = qkv
    cw = cw_ref[...]
    conv = qkv * cw[K - 1:K]
    for j in range(K - 1):
        off = CONV_PAD - (K - 1) + j
        conv = conv + xp_scr[off:off + tm, :] * cw[j:j + 1]
    tail = qkv[tm - (K - 1):tm, :]
    xp_scr[CONV_PAD - (K - 1):CONV_PAD, :] = tail

    @pl.when(t == nt - 1)
    def _():
        nh_ref[...] = tail

    cv_scr[0:tm, :] = _silu(conv)
    if rows > tm:
        cv_scr[tm:rows, :] = jnp.zeros((rows - tm, GDN_QKV), F32)

    ab = _dot(h, wab_ref[...])
    g = -jnp.exp(alog_ref[...]) * _softplus(ab[:, :LANES] + dtb_ref[...])
    beta = _sigmoid(ab[:, LANES:])
    if rows > tm:
        zpad = jnp.zeros((rows - tm, LANES), F32)
        g = jnp.concatenate([g, zpad], axis=0)
        beta = jnp.concatenate([beta, zpad], axis=0)

    ri = lax.broadcasted_iota(jnp.int32, (rows, rows), 0)
    ci = lax.broadcasted_iota(jnp.int32, (rows, rows), 1)
    tril = jnp.where((ri >= ci) & ((ri // C) == (ci // C)), 1.0, 0.0).astype(F32)
    gc = jnp.dot(tril, g, precision=HIGHEST, preferred_element_type=F32)
    gct = gc.T

    ii = lax.broadcasted_iota(jnp.int32, (C, C), 0)
    jj = lax.broadcasted_iota(jnp.int32, (C, C), 1)
    same = []
    s = 8
    while s <= C:
        same.append((ii // s) == (jj // s))
        s *= 2
    scale = GDN_DK ** -0.5
    HK = GDN_HEADS * GDN_DK

    heads = range(GDN_HEADS)
    on = on_ref[...]
    S = [s_scr[hd] for hd in heads]
    for c in range(rows // C):
        rs = slice(c * C, (c + 1) * C)
        q, k, v, gcol, bcol, eg, kb, A, Ab, qk, N, P = ([None] * GDN_HEADS for _ in range(12))
        for hd in heads:
            qh = cv_scr[rs, hd * GDN_DK:(hd + 1) * GDN_DK]
            kh = cv_scr[rs, HK + hd * GDN_DK:HK + (hd + 1) * GDN_DK]
            v[hd] = cv_scr[rs, 2 * HK + hd * GDN_DV:2 * HK + (hd + 1) * GDN_DV]
            q[hd] = qh * lax.rsqrt(jnp.sum(qh * qh, axis=-1, keepdims=True) + EPS) * scale
            k[hd] = kh * lax.rsqrt(jnp.sum(kh * kh, axis=-1, keepdims=True) + EPS)
            gcol[hd] = gc[rs, hd:hd + 1]
            grow = gct[hd:hd + 1, rs]
            bcol[hd] = beta[rs, hd:hd + 1]
            decay = jnp.exp(jnp.where(ii >= jj, gcol[hd] - grow, NEG))
            kb[hd] = k[hd] * bcol[hd]
            kq = _dot_t(jnp.concatenate([kb[hd], q[hd]], axis=0).astype(BF16), k[hd].astype(BF16))
            A[hd] = jnp.where(ii > jj, kq[:C] * decay, 0.0)
            qk[hd] = (kq[C:] * decay).astype(BF16)
            Ab[hd] = A[hd].astype(BF16)
        for hd in heads:
            D = jnp.where(same[0], Ab[hd], 0)
            N[hd] = -jnp.where(same[0], A[hd], 0.0)
            P[hd] = _dot(D, D)
        for hd in heads:
            Pb = P[hd].astype(BF16)
            N[hd] = N[hd] + P[hd] + _dot(N[hd].astype(BF16), Pb)
            P[hd] = _dot(Pb, Pb)
        for hd in heads:
            N[hd] = N[hd] + P[hd] + _dot(N[hd].astype(BF16), P[hd].astype(BF16))
        for lvl in range(1, len(same)):
            sel = same[lvl] & jnp.logical_not(same[lvl - 1])
            X = [None] * GDN_HEADS
            for hd in heads:
                off = jnp.where(sel, Ab[hd], 0)
                X[hd] = off.astype(F32) + _dot(off, N[hd].astype(BF16))
            for hd in heads:
                N[hd] = N[hd] - X[hd] - _dot(N[hd].astype(BF16), X[hd].astype(BF16))
        w_v, w_k = [None] * GDN_HEADS, [None] * GDN_HEADS
        for hd in heads:
            eg[hd] = jnp.exp(gcol[hd])
            rhs = jnp.concatenate([v[hd] * bcol[hd], kb[hd] * eg[hd]], axis=1)
            sol = rhs + _dot(N[hd].astype(BF16), rhs.astype(BF16))
            w_v[hd], w_k[hd] = sol[:, :GDN_DV], sol[:, GDN_DV:].astype(BF16)
        Sb = [S[hd].astype(BF16) for hd in heads]
        ub = [(w_v[hd] - _dot(w_k[hd], Sb[hd])).astype(BF16) for hd in heads]
        for hd in heads:
            o = _dot((q[hd] * eg[hd]).astype(BF16), Sb[hd]) + _dot(qk[hd], ub[hd])
            gl = gcol[hd][C - 1:C, :]
            kdec = (k[hd] * jnp.exp(gl - gcol[hd])).T.astype(BF16)
            S[hd] = S[hd] * jnp.exp(gl) + _dot(kdec, ub[hd])
            o_scr[rs, hd * GDN_DV:(hd + 1) * GDN_DV] = _rms(o, on)
    for hd in heads:
        s_scr[hd] = S[hd]

    z = _dot(h, wz_ref[...])
    og = (o_scr[0:tm, :] * _silu(z)).astype(BF16)
    xo_ref[...] = x + gt_ref[...] * _dot(og, wo_ref[...])

    @pl.when(t == nt - 1)
    def _():
        so_ref[...] = s_scr[...]


def _gdn_layer(x2d, nb, T, sc, sh, gt, ng, hist, s0, w_in, conv_w, a_log, dt_bias, o_norm, w_o, tm):
    nt = T // tm
    rows = max(tm, GDN_CHUNK)
    H, dv = GDN_HEADS, GDN_DV
    wqkv = w_in[:, :GDN_QKV].astype(BF16)
    wz = w_in[:, GDN_QKV:GDN_QKV + H * dv].astype(BF16)
    wa = w_in[:, GDN_QKV + H * dv:GDN_QKV + H * dv + H]
    wb = w_in[:, GDN_QKV + H * dv + H:]
    lane_pad = ((0, 0), (0, LANES - H))
    wab = jnp.concatenate([jnp.pad(wa, lane_pad), jnp.pad(wb, lane_pad)], axis=1).astype(BF16)
    alog = jnp.pad(a_log.reshape(1, H), lane_pad)
    dtb = jnp.pad(dt_bias.reshape(1, H), lane_pad)
    kern = functools.partial(_gdn_kernel, tm=tm, rows=rows)
    return pl.pallas_call(
        kern,
        grid=(nb, nt),
        in_specs=[_row_spec(tm, D_MODEL, nt), _batch_vec_spec(D_MODEL), _batch_vec_spec(D_MODEL),
                  _batch_vec_spec(D_MODEL), _const_spec((1, D_MODEL)),
                  pl.BlockSpec((None, GDN_CONV - 1, GDN_QKV), lambda b, t: (b, 0, 0)),
                  pl.BlockSpec((None, H, GDN_DK, dv), lambda b, t: (b, 0, 0, 0)),
                  _const_spec((D_MODEL, GDN_QKV)), _const_spec((D_MODEL, H * dv)),
                  _const_spec((D_MODEL, 2 * LANES)), _const_spec((GDN_CONV, GDN_QKV)),
                  _const_spec((1, LANES)), _const_spec((1, LANES)), _const_spec((1, dv)),
                  _const_spec((H * dv, D_MODEL))],
        out_specs=[_row_spec(tm, D_MODEL, nt),
                   pl.BlockSpec((None, GDN_CONV - 1, GDN_QKV), lambda b, t: (b, 0, 0)),
                   pl.BlockSpec((None, H, GDN_DK, dv), lambda b, t: (b, 0, 0, 0))],
        out_shape=[jax.ShapeDtypeStruct((nb * T, D_MODEL), F32),
                   jax.ShapeDtypeStruct((nb, GDN_CONV - 1, GDN_QKV), F32),
                   jax.ShapeDtypeStruct((nb, H, GDN_DK, dv), F32)],
        scratch_shapes=[pltpu.VMEM((CONV_PAD + tm, GDN_QKV), F32),
                        pltpu.VMEM((rows, GDN_QKV), F32),
                        pltpu.VMEM((H, GDN_DK, dv), F32),
                        pltpu.VMEM((rows, H * dv), F32)],
        compiler_params=_cp(2),
        name="gdn_layer",
    )(x2d, sc, sh, gt, ng.reshape(1, D_MODEL), hist, s0, wqkv, wz, wab, conv_w, alog, dtb,
      o_norm.reshape(1, dv), w_o.astype(BF16))


def _ffn_kernel(x_ref, sc_ref, sh_ref, gt_ref, ng_ref, hist_ref, wg_ref, wu_ref, cw_ref, cb_ref, wo_ref, fn_ref,
                xo_ref, nh_ref, xp_scr, *, tm, final):
    t = pl.program_id(1)
    nt = pl.num_programs(1)
    K = FFN_CONV

    @pl.when(t == 0)
    def _():
        xp_scr[CONV_PAD - (K - 1):CONV_PAD, :] = hist_ref[...]

    x = x_ref[...]
    h = _normmod(x, ng_ref[...], sc_ref[...], sh_ref[...]).astype(BF16)
    gate = _dot(h, wg_ref[...])
    xp_scr[CONV_PAD:CONV_PAD + tm, :] = gate
    cw = cw_ref[...]
    conv = gate * cw[K - 1:K]
    for j in range(K - 1):
        off = CONV_PAD - (K - 1) + j
        conv = conv + xp_scr[off:off + tm, :] * cw[j:j + 1]
    tail = gate[tm - (K - 1):tm, :]
    xp_scr[CONV_PAD - (K - 1):CONV_PAD, :] = tail

    @pl.when(t == nt - 1)
    def _():
        nh_ref[...] = tail

    up = _dot(h, wu_ref[...])
    act = (_silu(conv + cb_ref[...]) * up).astype(BF16)
    y = x + gt_ref[...] * _dot(act, wo_ref[...])
    if final:
        y = _rms(y, fn_ref[...])
    xo_ref[...] = y


def _ffn_layer(x2d, nb, T, sc, sh, gt, ng, hist, w_in, conv_w, conv_b, w_out, final_g, final, tm):
    nt = T // tm
    wg = w_in[:, :D_FF].astype(BF16)
    wu = w_in[:, D_FF:].astype(BF16)
    kern = functools.partial(_ffn_kernel, tm=tm, final=final)
    return pl.pallas_call(
        kern,
        grid=(nb, nt),
        in_specs=[_row_spec(tm, D_MODEL, nt), _batch_vec_spec(D_MODEL), _batch_vec_spec(D_MODEL),
                  _batch_vec_spec(D_MODEL), _const_spec((1, D_MODEL)),
                  pl.BlockSpec((None, FFN_CONV - 1, D_FF), lambda b, t: (b, 0, 0)),
                  _resident_spec((D_MODEL, D_FF)), _resident_spec((D_MODEL, D_FF)),
                  _const_spec((FFN_CONV, D_FF)), _const_spec((1, D_FF)),
                  _resident_spec((D_FF, D_MODEL)), _const_spec((1, D_MODEL))],
        out_specs=[_row_spec(tm, D_MODEL, nt),
                   pl.BlockSpec((None, FFN_CONV - 1, D_FF), lambda b, t: (b, 0, 0))],
        out_shape=[jax.ShapeDtypeStruct((nb * T, D_MODEL), F32),
                   jax.ShapeDtypeStruct((nb, FFN_CONV - 1, D_FF), F32)],
        scratch_shapes=[pltpu.VMEM((CONV_PAD + tm, D_FF), F32)],
        compiler_params=_cp(2),
        name="conv_ffn_layer",
    )(x2d, sc, sh, gt, ng.reshape(1, D_MODEL), hist, wg, wu, conv_w, conv_b.reshape(1, D_FF),
      w_out.astype(BF16), final_g.reshape(1, D_MODEL))


def _proj_res_kernel(a_ref, w_ref, x_ref, gt_ref, o_ref):
    o_ref[...] = x_ref[...] + gt_ref[...] * _dot(a_ref[...], w_ref[...])


def _proj_res(a2d, w, x2d, gt, nb, T, tm):
    nt = T // tm
    kdim = a2d.shape[1]
    return pl.pallas_call(
        _proj_res_kernel,
        grid=(nb, nt),
        in_specs=[_row_spec(tm, kdim, nt), _const_spec((kdim, D_MODEL)), _row_spec(tm, D_MODEL, nt),
                  _batch_vec_spec(D_MODEL)],
        out_specs=_row_spec(tm, D_MODEL, nt),
        out_shape=jax.ShapeDtypeStruct((nb * T, D_MODEL), F32),
        compiler_params=_cp(2),
        name="attn_out_proj",
    )(a2d, w.astype(BF16), x2d, gt)


def _mla_q_kernel(x_ref, sc_ref, sh_ref, ng_ref, wq_ref, wkv_ref, wr_ref, qn_ref, kvn_ref, wu1_ref, wu2_ref,
                  cos_ref, sin_ref, q_ref, ckv_ref, kr_ref):
    h = _normmod(x_ref[...], ng_ref[...], sc_ref[...], sh_ref[...]).astype(BF16)
    cq = _rms(_dot(h, wq_ref[...]), qn_ref[...]).astype(BF16)
    ckv_ref[...] = _rms(_dot(h, wkv_ref[...]), kvn_ref[...])
    cs = cos_ref[...]
    sn = sin_ref[...]
    r = _dot(h, wr_ref[...])
    kr_ref[...] = r[:, :MLA_SLAB] * cs + r[:, MLA_SLAB:] * sn
    q1 = _dot(cq, wu1_ref[...])
    q2 = _dot(cq, wu2_ref[...])
    for hd in range(MLA_HEADS):
        sl = slice(hd * MLA_SLAB, (hd + 1) * MLA_SLAB)
        q_ref[:, sl] = (q1[:, sl] * cs + q2[:, sl] * sn).astype(BF16)


def _rope_tables(pos):
    half = MLA_ROPE // 2
    inv = ROPE_THETA ** (-jnp.arange(half, dtype=F32) / half)
    ang = pos.astype(F32)[:, None] * inv[None, :]
    cos, sin = jnp.cos(ang), jnp.sin(ang)
    n = pos.shape[0]
    pad = jnp.zeros((n, MLA_SLAB - MLA_NOPE - MLA_ROPE), F32)
    cs = jnp.concatenate([jnp.ones((n, MLA_NOPE), F32), cos, cos, pad], axis=1)
    sn = jnp.concatenate([jnp.zeros((n, MLA_NOPE), F32), -sin, sin, pad], axis=1)
    return cs, sn


def _slab_weights(w_q_up, w_kv_up, w_r):
    half = MLA_ROPE // 2
    hq = MLA_NOPE + MLA_ROPE
    padw = MLA_SLAB - hq
    wq = w_q_up.reshape(MLA_Q_LORA, MLA_HEADS, hq)
    wq_n, wq_r = wq[..., :MLA_NOPE], wq[..., MLA_NOPE:]
    wq_sw = jnp.concatenate([wq_r[..., half:], wq_r[..., :half]], axis=-1)
    zq = jnp.zeros((MLA_Q_LORA, MLA_HEADS, padw), F32)
    wu1 = jnp.concatenate([wq_n, wq_r, zq], axis=-1).reshape(MLA_Q_LORA, MLA_HEADS * MLA_SLAB)
    wu2 = jnp.concatenate([jnp.zeros_like(wq_n), wq_sw, zq], axis=-1).reshape(MLA_Q_LORA, MLA_HEADS * MLA_SLAB)
    wkv = w_kv_up.reshape(MLA_KV_LORA, MLA_HEADS, MLA_NOPE + MLA_V)
    wk = jnp.concatenate([wkv[..., :MLA_NOPE], jnp.zeros((MLA_KV_LORA, MLA_HEADS, MLA_SLAB - MLA_NOPE), F32)],
                         axis=-1).reshape(MLA_KV_LORA, MLA_HEADS * MLA_SLAB)
    wv = wkv[..., MLA_NOPE:].reshape(MLA_KV_LORA, MLA_HEADS * MLA_V)
    d = w_r.shape[0]
    wr_sw = jnp.concatenate([w_r[:, half:], w_r[:, :half]], axis=1)
    zl = jnp.zeros((d, MLA_NOPE), F32)
    zr = jnp.zeros((d, padw), F32)
    wr2 = jnp.concatenate([zl, w_r, zr, zl, wr_sw, zr], axis=1)
    return wu1.astype(BF16), wu2.astype(BF16), wk.astype(BF16), wv.astype(BF16), wr2.astype(BF16)


def _mla_q(x2d, nb, T, sc, sh, ng, wq, wkv, wr2, qn, kvn, wu1, wu2, cs, sn, tm):
    nt = T // tm
    qw = MLA_HEADS * MLA_SLAB
    return pl.pallas_call(
        _mla_q_kernel,
        grid=(nb, nt),
        in_specs=[_row_spec(tm, D_MODEL, nt), _batch_vec_spec(D_MODEL), _batch_vec_spec(D_MODEL),
                  _const_spec((1, D_MODEL)), _const_spec((D_MODEL, MLA_Q_LORA)),
                  _const_spec((D_MODEL, MLA_KV_LORA)), _const_spec((D_MODEL, 2 * MLA_SLAB)),
                  _const_spec((1, MLA_Q_LORA)), _const_spec((1, MLA_KV_LORA)),
                  _const_spec((MLA_Q_LORA, qw)), _const_spec((MLA_Q_LORA, qw)),
                  pl.BlockSpec((tm, MLA_SLAB), lambda b, t: (t, 0)),
                  pl.BlockSpec((tm, MLA_SLAB), lambda b, t: (t, 0))],
        out_specs=[_row_spec(tm, qw, nt), _row_spec(tm, MLA_KV_LORA, nt), _row_spec(tm, MLA_SLAB, nt)],
        out_shape=[jax.ShapeDtypeStruct((nb * T, qw), BF16),
                   jax.ShapeDtypeStruct((nb * T, MLA_KV_LORA), F32),
                   jax.ShapeDtypeStruct((nb * T, MLA_SLAB), F32)],
        compiler_params=_cp(2),
        name="mla_q_proj",
    )(x2d, sc, sh, ng.reshape(1, D_MODEL), wq, wkv, wr2, qn.reshape(1, -1), kvn.reshape(1, -1), wu1, wu2, cs, sn)


def _mla_kv_kernel(ckv_ref, kr_ref, wk_ref, wvt_ref, k_ref, vt_ref):
    c = ckv_ref[...].astype(BF16)
    kn = _dot(c, wk_ref[...])
    kr = kr_ref[...]
    for hd in range(MLA_HEADS):
        sl = slice(hd * MLA_SLAB, (hd + 1) * MLA_SLAB)
        k_ref[:, sl] = (kn[:, sl] + kr).astype(BF16)
    vt_ref[...] = _dot_t(wvt_ref[...], c).astype(BF16)


def _mla_kv(ckv2d, kr2d, wk, wvt, nb, Tk, tm):
    nt = Tk // tm
    kw = MLA_HEADS * MLA_SLAB
    vw = MLA_HEADS * MLA_V
    return pl.pallas_call(
        _mla_kv_kernel,
        grid=(nb, nt),
        in_specs=[_row_spec(tm, MLA_KV_LORA, nt), _row_spec(tm, MLA_SLAB, nt),
                  _const_spec((MLA_KV_LORA, kw)), _const_spec((vw, MLA_KV_LORA))],
        out_specs=[_row_spec(tm, kw, nt), pl.BlockSpec((None, vw, tm), lambda b, t: (b, 0, t))],
        out_shape=[jax.ShapeDtypeStruct((nb * Tk, kw), BF16), jax.ShapeDtypeStruct((nb, vw, Tk), BF16)],
        compiler_params=_cp(2),
        name="mla_kv_up",
    )(ckv2d, kr2d, wk, wvt)


def _mla_attn_kernel(q_ref, k_ref, vt_ref, o_ref, *, tq, tk, causal, n_kv_full):
    i = pl.program_id(2)
    scale = (MLA_NOPE + MLA_ROPE) ** -0.5
    heads = range(MLA_GROUP)
    qs = [q_ref[:, hh * MLA_SLAB:(hh + 1) * MLA_SLAB] for hh in heads]

    def tile(off, carry, masked):
        if masked:
            krow = lax.broadcasted_iota(jnp.int32, (tk, tq), 0)
            qcol = lax.broadcasted_iota(jnp.int32, (tk, tq), 1)
            keep = (krow // CHUNK) <= (qcol // CHUNK)
        s = [_dot_t(k_ref[pl.ds(off, tk), hh * MLA_SLAB:(hh + 1) * MLA_SLAB], qs[hh]) * scale for hh in heads]
        if masked:
            s = [jnp.where(keep, s[hh], NEG) for hh in heads]
        m_new = [jnp.maximum(carry[hh][0], jnp.max(s[hh], axis=0, keepdims=True)) for hh in heads]
        p = [jnp.exp(s[hh] - m_new[hh]) for hh in heads]
        out = []
        for hh in heads:
            m, l, acc = carry[hh]
            alpha = jnp.exp(m - m_new[hh])
            l = alpha * l + jnp.sum(p[hh], axis=0, keepdims=True)
            vt = vt_ref[hh * MLA_V:(hh + 1) * MLA_V, pl.ds(off, tk)]
            acc = alpha * acc + _dot(vt, p[hh].astype(BF16))
            out.append((m_new[hh], l, acc))
        return tuple(out)

    init = tuple((jnp.full((1, tq), NEG, F32), jnp.zeros((1, tq), F32), jnp.zeros((MLA_V, tq), F32))
                 for _ in heads)
    n_open = i if causal else n_kv_full
    res = lax.fori_loop(0, n_open, lambda j, c: tile(pl.multiple_of(j * tk, tk), c, False), init)
    if causal:
        res = tile(pl.multiple_of(i * tq, tq), res, True)
    ot = jnp.concatenate([res[hh][2] / res[hh][1] for hh in heads], axis=0).astype(BF16)
    r = lax.broadcasted_iota(jnp.int32, (tq, tq), 0)
    c = lax.broadcasted_iota(jnp.int32, (tq, tq), 1)
    eye = jnp.where(r == c, 1.0, 0.0).astype(BF16)
    o_ref[...] = _dot_t(eye, ot).astype(BF16)


def _mla_attn(q3, k3, vt3, tq, tk, causal):
    nb, Tq, _ = q3.shape
    Tk = k3.shape[1]
    G = MLA_GROUP
    kern = functools.partial(_mla_attn_kernel, tq=tq, tk=tk, causal=causal, n_kv_full=Tk // tk)
    return pl.pallas_call(
        kern,
        grid=(nb, MLA_HEADS // G, Tq // tq),
        in_specs=[pl.BlockSpec((None, tq, G * MLA_SLAB), lambda b, p, i: (b, i, p)),
                  pl.BlockSpec((None, Tk, G * MLA_SLAB), lambda b, p, i: (b, 0, p)),
                  pl.BlockSpec((None, G * MLA_V, Tk), lambda b, p, i: (b, p, 0))],
        out_specs=pl.BlockSpec((None, tq, G * MLA_V), lambda b, p, i: (b, i, p)),
        out_shape=jax.ShapeDtypeStruct((nb, Tq, MLA_HEADS * MLA_V), BF16),
        compiler_params=_cp(3),
        name="mla_attention",
    )(q3, k3, vt3)


def _swa_proj_kernel(x_ref, sc_ref, sh_ref, ng_ref, w_ref, q_ref, kv_ref, last_ref, *, tm, keep):
    h = _normmod(x_ref[...], ng_ref[...], sc_ref[...], sh_ref[...]).astype(BF16)
    proj = _dot(h, w_ref[...])
    nq = SWA_HEADS * SWA_HD
    q_ref[...] = proj[:, :nq].astype(BF16)
    kv = proj[:, nq:]
    kv_ref[...] = kv.astype(BF16)
    last_ref[...] = kv[tm - keep:, :]


def _swa_proj(x2d, nb, T, sc, sh, ng, w_in, tm):
    nt = T // tm
    keep = min(WINDOW, T)
    nq = SWA_HEADS * SWA_HD
    nkv = 2 * SWA_KV_HEADS * SWA_HD
    kern = functools.partial(_swa_proj_kernel, tm=tm, keep=keep)
    return pl.pallas_call(
        kern,
        grid=(nb, nt),
        in_specs=[_row_spec(tm, D_MODEL, nt), _batch_vec_spec(D_MODEL), _batch_vec_spec(D_MODEL),
                  _const_spec((1, D_MODEL)), _const_spec((D_MODEL, nq + nkv))],
        out_specs=[_row_spec(tm, nq, nt), _row_spec(tm, nkv, nt),
                   pl.BlockSpec((None, keep, nkv), lambda b, t: (b, 0, 0))],
        out_shape=[jax.ShapeDtypeStruct((nb * T, nq), BF16), jax.ShapeDtypeStruct((nb * T, nkv), BF16),
                   jax.ShapeDtypeStruct((nb, keep, nkv), F32)],
        compiler_params=_cp(2),
        name="swa_proj",
    )(x2d, sc, sh, ng.reshape(1, D_MODEL), w_in.astype(BF16))


def _t5_bias_kernel(bucket_ref, table_ref, o_ref):
    hd = pl.program_id(0)
    bucket = bucket_ref[...]
    acc = jnp.full(bucket.shape, NEG, F32)
    for b in range(REL_BUCKETS):
        acc = jnp.where(bucket == b, table_ref[b, hd], acc)
    o_ref[...] = acc


def _t5_bias(rel_bias, q_pos, k_pos, valid):
    n = q_pos[:, None] - k_pos[None, :]
    half = REL_BUCKETS // 2
    exact = half // 2
    side = jnp.where(n < 0, half, 0)
    n = jnp.abs(n)
    log_b = exact + (jnp.log(jnp.maximum(n, 1).astype(F32) / exact)
                     / math.log(REL_MAX_DIST / exact) * (half - exact)).astype(jnp.int32)
    bucket = side + jnp.where(n < exact, n, jnp.minimum(log_b, half - 1))
    bucket = jnp.where(valid, bucket, -1).astype(jnp.int32)
    tq, tk = bucket.shape
    return pl.pallas_call(
        _t5_bias_kernel,
        grid=(SWA_HEADS,),
        in_specs=[pl.BlockSpec((tq, tk), lambda h: (0, 0)),
                  pl.BlockSpec(memory_space=pltpu.SMEM)],
        out_specs=pl.BlockSpec((None, tq, tk), lambda h: (h, 0, 0)),
        out_shape=jax.ShapeDtypeStruct((SWA_HEADS, tq, tk), F32),
        compiler_params=_cp(1),
        name="t5_bias",
    )(bucket, rel_bias)


def _swa_attn_kernel(q_ref, kv_ref, bias_ref, sink_ref, o_ref, *, tq, lk, npad):
    i = pl.program_id(1)
    off = pl.multiple_of(i * tq, tq)
    kvw = kv_ref[pl.ds(off, lk), :]
    nk = SWA_KV_HEADS * SWA_HD
    q = q_ref[...]
    if npad:
        col = lax.broadcasted_iota(jnp.int32, (tq, lk), 1) + off
        real = col >= npad
    outs = []
    for hd in range(SWA_HEADS):
        kvh = hd // SWA_GROUP
        qh = q[:, hd * SWA_HD:(hd + 1) * SWA_HD]
        kh = kvw[:, kvh * SWA_HD:(kvh + 1) * SWA_HD]
        vh = kvw[:, nk + kvh * SWA_HD:nk + (kvh + 1) * SWA_HD]
        s = _dot_t(qh, kh) * (SWA_HD ** -0.5) + bias_ref[hd]
        if npad:
            s = jnp.where(real, s, NEG)
        sink = sink_ref[hd]
        m = jnp.maximum(jnp.max(s, axis=-1, keepdims=True), sink)
        p = jnp.exp(s - m)
        p = p / (jnp.sum(p, axis=-1, keepdims=True) + jnp.exp(sink - m))
        outs.append(_dot(p.astype(BF16), vh))
    o_ref[...] = jnp.concatenate(outs, axis=-1).astype(BF16)


def _swa_attn(q3, kv3, bias, sinks, tq, lk, npad):
    nb, Tq, nq = q3.shape
    Tk, nkv = kv3.shape[1:]
    kern = functools.partial(_swa_attn_kernel, tq=tq, lk=lk, npad=npad)
    return pl.pallas_call(
        kern,
        grid=(nb, Tq // tq),
        in_specs=[pl.BlockSpec((None, tq, nq), lambda b, i: (b, i, 0)),
                  pl.BlockSpec((None, Tk, nkv), lambda b, i: (b, 0, 0)),
                  _const_spec((SWA_HEADS, tq, lk)),
                  pl.BlockSpec(memory_space=pltpu.SMEM)],
        out_specs=pl.BlockSpec((None, tq, nq), lambda b, i: (b, i, 0)),
        out_shape=jax.ShapeDtypeStruct((nb, Tq, nq), BF16),
        compiler_params=_cp(2),
        name="swa_attention",
    )(q3, kv3, bias, sinks)


def _trunk(x, mods, prompt, past_len, st, p, tm, wide_tm):
    B, T, _ = x.shape
    x2d = x.reshape(B * T, D_MODEL)
    pos = jnp.arange(T, dtype=jnp.int32) + (0 if prompt else past_len)
    new = dict(gdn_conv=[], gdn_S=[], mla_latent=[], mla_krope=[], swa_k=[], swa_v=[], ffn_conv=[])
    depth = p['ada_w'].shape[0]
    for layer in range(depth):
        kind, slot = layer % 3, layer // 3
        sh1, sc1, g1, sh2, sc2, g2 = [mods[layer, :, j * D_MODEL:(j + 1) * D_MODEL][:, None, :] for j in range(6)]
        ng = p['norm1'][layer]
        if kind == 0:
            x2d, conv_h, S = _gdn_layer(x2d, B, T, sc1, sh1, g1, ng, st['gdn_conv'][slot], st['gdn_S'][slot],
                                        p['gdn_w_in'][slot], p['gdn_conv_w'][slot], p['gdn_a_log'][slot],
                                        p['gdn_dt_bias'][slot], p['gdn_o_norm'][slot], p['gdn_w_o'][slot], tm)
            new['gdn_conv'].append(conv_h)
            new['gdn_S'].append(S)
        elif kind == 1:
            w_in = p['mla_w_in'][slot]
            wq = w_in[:, :MLA_Q_LORA].astype(BF16)
            wkv = w_in[:, MLA_Q_LORA:MLA_Q_LORA + MLA_KV_LORA].astype(BF16)
            wu1, wu2, wk, wv, wr2 = _slab_weights(p['mla_w_q_up'][slot], p['mla_w_kv_up'][slot],
                                                  w_in[:, MLA_Q_LORA + MLA_KV_LORA:])
            cs, sn = _rope_tables(pos)
            qcat, ckv, kr = _mla_q(x2d, B, T, sc1, sh1, ng, wq, wkv, wr2, p['mla_q_norm'][slot],
                                   p['mla_kv_norm'][slot], wu1, wu2, cs, sn, wide_tm)
            ckv3 = ckv.reshape(B, T, MLA_KV_LORA)
            kr3 = kr.reshape(B, T, MLA_SLAB)
            if prompt:
                ckv_all, kr_all, Tk = ckv, kr, T
                kv_tm, tq, tk = tm, tm, tm
            else:
                assert past_len % CHUNK + T <= CHUNK
                cache_kr = jnp.pad(st['mla_krope'][slot].astype(F32),
                                   ((0, 0), (0, 0), (MLA_NOPE, MLA_SLAB - MLA_NOPE - MLA_ROPE)))
                Tk = past_len + T
                ckv_all = jnp.concatenate([st['mla_latent'][slot].astype(F32), ckv3], axis=1).reshape(B * Tk, -1)
                kr_all = jnp.concatenate([cache_kr, kr3], axis=1).reshape(B * Tk, MLA_SLAB)
                kv_tm, tq, tk = Tk, T, Tk
            kcat, vt = _mla_kv(ckv_all, kr_all, wk, wv.T, B, Tk, kv_tm)
            o = _mla_attn(qcat.reshape(B, T, -1), kcat.reshape(B, Tk, -1), vt, tq, tk, prompt)
            x2d = _proj_res(o.reshape(B * T, -1), p['mla_w_o'][slot], x2d, g1, B, T, wide_tm)
            new['mla_latent'].append(ckv3)
            new['mla_krope'].append(kr3[:, :, MLA_NOPE:MLA_NOPE + MLA_ROPE])
        else:
            q, kvb, kv_last = _swa_proj(x2d, B, T, sc1, sh1, ng, p['swa_w_in'][slot], wide_tm)
            nk = SWA_KV_HEADS * SWA_HD
            if prompt:
                tq = 2 * CHUNK
                lk = tq + WINDOW
                rq = WINDOW + jnp.arange(tq)
                rk = jnp.arange(lk)
                d = rq[:, None] // CHUNK - rk[None, :] // CHUNK
                bias = _t5_bias(p['rel_bias'], rq, rk, (d >= 0) & (d <= WIN_CHUNKS))
                kv3 = jnp.pad(kvb.reshape(B, T, 2 * nk), ((0, 0), (WINDOW, 0), (0, 0)))
                heads = _swa_attn(q.reshape(B, T, -1), kv3, bias, p['swa_sinks'][slot], tq, lk, WINDOW)
                new_k, new_v = kv_last[:, :, :nk], kv_last[:, :, nk:]
            else:
                win = st['swa_k'].shape[2]
                ck = st['swa_k'][slot].reshape(B, win, nk).astype(F32)
                cv = st['swa_v'][slot].reshape(B, win, nk).astype(F32)
                k_all = jnp.concatenate([ck, kv_last[:, :, :nk]], axis=1)
                v_all = jnp.concatenate([cv, kv_last[:, :, nk:]], axis=1)
                k_pos = jnp.concatenate([past_len - win + jnp.arange(win, dtype=jnp.int32), pos])
                d = (pos // CHUNK)[:, None] - (k_pos // CHUNK)[None, :]
                bias = _t5_bias(p['rel_bias'], pos, k_pos, (d >= 0) & (d <= WIN_CHUNKS))
                kv3 = jnp.concatenate([k_all, v_all], axis=-1).astype(BF16)
                heads = _swa_attn(q.reshape(B, T, -1), kv3, bias, p['swa_sinks'][slot], T, win + T, 0)
                new_k, new_v = k_all[:, -win:], v_all[:, -win:]
            x2d = _proj_res(heads.reshape(B * T, -1), p['swa_w_o'][slot], x2d, g1, B, T, wide_tm)
            new['swa_k'].append(new_k.reshape(B, -1, SWA_KV_HEADS, SWA_HD))
            new['swa_v'].append(new_v.reshape(B, -1, SWA_KV_HEADS, SWA_HD))
        x2d, f_hist = _ffn_layer(x2d, B, T, sc2, sh2, g2, p['norm2'][layer], st['ffn_conv'][layer],
                                 p['ffn_w_in'][layer], p['ffn_conv_w'][layer], p['ffn_conv_b'][layer],
                                 p['ffn_w_out'][layer], p['final_norm'], layer == depth - 1, wide_tm)
        new['ffn_conv'].append(f_hist)
    return x2d.reshape(B, T, D_MODEL), {name: jnp.stack(rows) for name, rows in new.items()}


def kernel(x_prompt, x_sample, c_prompt, c_sample, state_gdn_conv, state_gdn_S, cache_mla_latent, cache_mla_krope, cache_swa_k, cache_swa_v, state_ffn_conv, ada_w, ada_b, norm1, norm2, final_norm, gdn_w_in, gdn_conv_w, gdn_a_log, gdn_dt_bias, gdn_o_norm, gdn_w_o, mla_w_in, mla_q_norm, mla_kv_norm, mla_w_q_up, mla_w_kv_up, mla_w_o, swa_w_in, swa_sinks, swa_w_o, rel_bias, ffn_w_in, ffn_conv_w, ffn_conv_b, ffn_w_out):
    p = dict(ada_w=ada_w, ada_b=ada_b, norm1=norm1, norm2=norm2, final_norm=final_norm,
             gdn_w_in=gdn_w_in, gdn_conv_w=gdn_conv_w, gdn_a_log=gdn_a_log, gdn_dt_bias=gdn_dt_bias,
             gdn_o_norm=gdn_o_norm, gdn_w_o=gdn_w_o, mla_w_in=mla_w_in, mla_q_norm=mla_q_norm,
             mla_kv_norm=mla_kv_norm, mla_w_q_up=mla_w_q_up, mla_w_kv_up=mla_w_kv_up, mla_w_o=mla_w_o,
             swa_w_in=swa_w_in, swa_sinks=swa_sinks, swa_w_o=swa_w_o, rel_bias=rel_bias,
             ffn_w_in=ffn_w_in, ffn_conv_w=ffn_conv_w, ffn_conv_b=ffn_conv_b, ffn_w_out=ffn_w_out)
    bp, tp = x_prompt.shape[:2]
    bs, ts = x_sample.shape[:2]
    n_gdn, n_ffn = state_gdn_conv.shape[0], state_ffn_conv.shape[0]
    st_prompt = dict(gdn_conv=jnp.zeros((n_gdn, bp, GDN_CONV - 1, GDN_QKV), F32),
                     gdn_S=jnp.zeros((n_gdn, bp, GDN_HEADS, GDN_DK, GDN_DV), F32),
                     ffn_conv=jnp.zeros((n_ffn, bp, FFN_CONV - 1, D_FF), F32))
    st_sample = dict(gdn_conv=state_gdn_conv, gdn_S=state_gdn_S, mla_latent=cache_mla_latent,
                     mla_krope=cache_mla_krope, swa_k=cache_swa_k, swa_v=cache_swa_v,
                     ffn_conv=state_ffn_conv)
    past_len = cache_mla_latent.shape[2]
    c_all = jnp.concatenate([c_prompt, c_sample], axis=0)
    mods = _modulation(c_all, ada_w, ada_b)
    y_prompt, sp = _trunk(x_prompt, mods[:, :bp], True, 0, st_prompt, p, min(256, tp), min(512, tp))
    y_sample, ss = _trunk(x_sample, mods[:, bp:bp + bs], False, past_len, st_sample, p, ts, ts)
    return (y_prompt, y_sample,
            sp['gdn_conv'], ss['gdn_conv'],
            sp['gdn_S'], ss['gdn_S'],
            sp['mla_latent'], ss['mla_latent'],
            sp['mla_krope'], ss['mla_krope'],
            sp['swa_k'], ss['swa_k'],
            sp['swa_v'], ss['swa_v'],
            sp['ffn_conv'], ss['ffn_conv'])
```
